```python
import math
import jax, jax.numpy as jnp
from jax import lax
import numpy as np

D_MODEL = 1024
BATCH = 16
SEQ = 4096
DEPTH = 2

HEAD_DIM = 64
N_Q_HEADS = 16
N_KV_HEADS = 4
GQA_GROUP = N_Q_HEADS // N_KV_HEADS
WINDOW = 128
ATTN_BLOCK = 128
ROPE_THETA = 10000.0
ATT_Q = N_Q_HEADS * HEAD_DIM
ATT_KV = N_KV_HEADS * HEAD_DIM
ATTN_COLS = ATT_Q + 2 * ATT_KV

RWKV_HEAD = 64
RWKV_HEADS = 16
RWKV_DIM = RWKV_HEADS * RWKV_HEAD
DECAY_LORA = 64
AAA_LORA = 64
MV_LORA = 32
GATE_LORA = 128
RWKV_GN_EPS = 64e-5
RWKV_COLS = 3 * RWKV_DIM + DECAY_LORA + AAA_LORA + GATE_LORA

GATE_COLS = 2 * D_MODEL
IN_COLS = ATTN_COLS + RWKV_COLS + GATE_COLS

N_EXPERTS = 32
TOP_K = 4
D_EXPERT = 1024
SWIGLU_LIMIT = 7.0
SWIGLU_ALPHA = 1.702
MOE_BLOCK = 256

LN_EPS = 1e-5
DEEPNORM_ALPHA = (2.0 * DEPTH) ** 0.25
DEEPNORM_BETA = (8.0 * DEPTH) ** -0.25

kernel_name = "hybrid_swa_rwkv7_moe_deepnorm"


def layer_norm(x, g, b):
    xf = x.astype(jnp.float32)
    mu = jnp.mean(xf, axis=-1, keepdims=True)
    var = jnp.mean(jnp.square(xf - mu), axis=-1, keepdims=True)
    return ((xf - mu) * lax.rsqrt(var + LN_EPS)).astype(x.dtype) * g + b


def rope_tables(seq, dtype):
    pos = jnp.arange(seq, dtype=jnp.float32)
    inv = ROPE_THETA ** (-jnp.arange(0, HEAD_DIM, 2, dtype=jnp.float32) / HEAD_DIM)
    ang = pos[:, None] * inv[None, :]
    return jnp.cos(ang)[:, None, :].astype(dtype), jnp.sin(ang)[:, None, :].astype(dtype)


def apply_rope(t, cos, sin):
    t1, t2 = t[..., : HEAD_DIM // 2], t[..., HEAD_DIM // 2:]
    return jnp.concatenate([t1 * cos - t2 * sin, t2 * cos + t1 * sin], axis=-1)


def sliding_window_attention(q, k, v, sinks):
    B, S = q.shape[0], q.shape[1]
    nb = S // ATTN_BLOCK
    qb = q.reshape(B, nb, ATTN_BLOCK, N_KV_HEADS, GQA_GROUP, HEAD_DIM)

    def band(t):
        tb = t.reshape(B, nb, ATTN_BLOCK, N_KV_HEADS, HEAD_DIM)
        prev = jnp.pad(tb[:, :-1], ((0, 0), (1, 0), (0, 0), (0, 0), (0, 0)))
        return jnp.concatenate([prev, tb], axis=2)

    kb, vb = band(k), band(v)
    scores = jnp.einsum('bnqhgd,bnkhd->bnhgqk', qb, kb).astype(jnp.float32) * (HEAD_DIM ** -0.5)
    blk = jnp.arange(nb, dtype=jnp.int32)
    qpos = blk[:, None, None] * ATTN_BLOCK + jnp.arange(ATTN_BLOCK, dtype=jnp.int32)[None, :, None]
    kpos = (blk[:, None, None] - 1) * ATTN_BLOCK + jnp.arange(2 * ATTN_BLOCK, dtype=jnp.int32)[None, None, :]
    diff = qpos - kpos
    valid = (diff >= 0) & (diff < WINDOW) & (kpos >= 0)
    scores = jnp.where(valid[None, :, None, None, :, :], scores, -jnp.inf)
    sink = jnp.broadcast_to(sinks.astype(jnp.float32).reshape(1, 1, N_KV_HEADS, GQA_GROUP, 1, 1),
                            scores.shape[:-1] + (1,))
    probs = jax.nn.softmax(jnp.concatenate([scores, sink], axis=-1), axis=-1)[..., :-1].astype(v.dtype)
    out = jnp.einsum('bnhgqk,bnkhd->bnqhgd', probs, vb)
    return out.reshape(B, S, ATT_Q)


def token_shift(t):
    return jnp.pad(t[:, :-1], ((0, 0), (1, 0), (0, 0)))


def wkv7_scan(r, decay, k, v, a, b):
    B, S, H, N = r.shape
    xs = tuple(jnp.swapaxes(t.astype(jnp.float32), 0, 1) for t in (r, decay, k, v, a, b))

    def step(state, inp):
        r_t, w_t, k_t, v_t, a_t, b_t = inp
        sa = jnp.einsum('bhij,bhj->bhi', state, a_t)
        state = state * w_t[:, :, None, :] + sa[..., None] * b_t[:, :, None, :] + v_t[..., None] * k_t[:, :, None, :]
        return state, jnp.einsum('bhij,bhj->bhi', state, r_t)

    init = jnp.zeros((B, H, N, N), jnp.float32)
    _, out = lax.scan(step, init, xs)
    return jnp.swapaxes(out, 0, 1)


def rwkv7_time_mix(p, mu, w0, w2, a0, a2, g2, k_k, k_a, r_k, lnx_w, lnx_b, w_o, v_first, v_res):
    B, S, _ = p.shape
    p = p + (token_shift(p) - p) * mu
    o1, o2, o3 = RWKV_DIM, 2 * RWKV_DIM, 3 * RWKV_DIM
    o4, o5 = o3 + DECAY_LORA, o3 + DECAY_LORA + AAA_LORA
    r, k, v = p[..., :o1], p[..., o1:o2], p[..., o2:o3]
    wl, al, gl = p[..., o3:o4], p[..., o4:o5], p[..., o5:]
    log_w = -jax.nn.softplus(-(w0 + jnp.tanh(wl) @ w2)) - 0.5
    decay = jnp.exp(-jnp.exp(log_w.astype(jnp.float32)))
    a = jax.nn.sigmoid(a0 + al @ a2)
    g = jax.nn.sigmoid(gl) @ g2
    if v_res is not None:
        v0, v1, v2 = v_res
        v = v + (v_first - v) * jax.nn.sigmoid(v0 + (v @ v1) @ v2)
    heads = lambda t: t.reshape(B, S, RWKV_HEADS, RWKV_HEAD)
    kk = heads(k * k_k).astype(jnp.float32)
    kk = kk / jnp.maximum(jnp.sqrt(jnp.sum(kk * kk, axis=-1, keepdims=True)), 1e-12)
    k = k * (1.0 + (a - 1.0) * k_a)
    rh, kh, vh, ah = heads(r), heads(k), heads(v), heads(a)
    out = wkv7_scan(rh, heads(decay), kh, vh, -kk, kk * ah.astype(jnp.float32))
    mean = jnp.mean(out, axis=-1, keepdims=True)
    var = jnp.mean(jnp.square(out - mean), axis=-1, keepdims=True)
    gn = ((out - mean) * lax.rsqrt(var + RWKV_GN_EPS)).reshape(B, S, RWKV_DIM).astype(p.dtype) * lnx_w + lnx_b
    bonus = (jnp.sum(rh * kh * r_k, axis=-1, keepdims=True) * vh).reshape(B, S, RWKV_DIM)
    return ((gn + bonus) * g) @ w_o, v


def moe_ffn(x, router_w, router_b, w1, b1, w2, b2):
    B, S, D = x.shape
    T = B * S
    A = T * TOP_K
    xf = x.reshape(T, D)
    logits = (xf @ router_w + router_b).astype(jnp.float32)
    top_vals, top_idx = lax.top_k(logits, TOP_K)
    gates = jax.nn.softmax(top_vals, axis=-1)
    e_flat = top_idx.reshape(A).astype(jnp.int32)
    tok_flat = jnp.broadcast_to(jnp.arange(T, dtype=jnp.int32)[:, None], (T, TOP_K)).reshape(A)
    gate_flat = gates.reshape(A)
    order = jnp.argsort(e_flat)
    e_sorted = e_flat[order]
    counts = jnp.bincount(e_flat, length=N_EXPERTS).astype(jnp.int32)
    starts = jnp.cumsum(counts) - counts
    padded = ((counts + MOE_BLOCK - 1) // MOE_BLOCK) * MOE_BLOCK
    pend = jnp.cumsum(padded)
    pstart = pend - padded
    dest = pstart[e_sorted] + (jnp.arange(A, dtype=jnp.int32) - starts[e_sorted])
    n_blocks = -(-A // MOE_BLOCK) + N_EXPERTS
    P = n_blocks * MOE_BLOCK
    row_tok = jnp.zeros((P,), jnp.int32).at[dest].set(tok_flat[order])
    row_gate = jnp.zeros((P,), x.dtype).at[dest].set(gate_flat[order].astype(x.dtype))
    block_start = jnp.arange(n_blocks, dtype=jnp.int32) * MOE_BLOCK
    block_exp = jnp.minimum(jnp.searchsorted(pend, block_start, side='right'), N_EXPERTS - 1).astype(jnp.int32)

    def expert_block(args):
        tok, gate, e = args
        h = xf[tok] @ w1[e] + b1[e]
        x_glu = jnp.minimum(h[:, ::2], SWIGLU_LIMIT)
        x_lin = jnp.clip(h[:, 1::2], -SWIGLU_LIMIT, SWIGLU_LIMIT)
        act = x_glu * jax.nn.sigmoid(SWIGLU_ALPHA * x_glu) * (x_lin + 1.0)
        return (act @ w2[e] + b2[e]) * gate[:, None]

    out = lax.map(expert_block, (row_tok.reshape(n_blocks, MOE_BLOCK), row_gate.reshape(n_blocks, MOE_BLOCK), block_exp))
    y = jax.ops.segment_sum(out.reshape(P, D), row_tok, num_segments=T)
    return y.reshape(B, S, D)


def setup_inputs(seed: int = 0) -> dict:
    key = jax.random.key(seed)
    ks = iter(jax.random.split(key, 40))
    nrm = lambda shape, scale: jax.random.normal(next(ks), shape, jnp.float32) * scale
    uni = lambda shape, lo, hi: jax.random.uniform(next(ks), shape, jnp.float32, lo, hi)
    L, D, Dr, E, I = DEPTH, D_MODEL, RWKV_DIM, N_EXPERTS, D_EXPERT
    Lv = max(DEPTH - 1, 0)
    return {
        "x": nrm((BATCH, SEQ, D), 1.0),
        "w_in": nrm((L, D, IN_COLS), D ** -0.5),
        "attn_sinks": nrm((L, N_Q_HEADS), 0.5),
        "w_attn_o": nrm((L, ATT_Q, D), ATT_Q ** -0.5),
        "rwkv_mu": uni((L, RWKV_COLS), 0.0, 1.0),
        "rwkv_w0": uni((L, Dr), -6.0, -1.0),
        "rwkv_w2": nrm((L, DECAY_LORA, Dr), 0.5 * DECAY_LORA ** -0.5),
        "rwkv_a0": nrm((L, Dr), 0.1),
        "rwkv_a2": nrm((L, AAA_LORA, Dr), 0.5 * AAA_LORA ** -0.5),
        "rwkv_g2": nrm((L, GATE_LORA, Dr), GATE_LORA ** -0.5),
        "rwkv_k_k": 0.85 + nrm((L, Dr), 0.05),
        "rwkv_k_a": 1.0 + nrm((L, Dr), 0.05),
        "rwkv_r_k": nrm((L, RWKV_HEADS, RWKV_HEAD), 0.1),
        "rwkv_v0": 1.0 + nrm((Lv, Dr), 0.1),
        "rwkv_v1": nrm((Lv, Dr, MV_LORA), Dr ** -0.5),
        "rwkv_v2": nrm((Lv, MV_LORA, Dr), 0.5 * MV_LORA ** -0.5),
        "rwkv_lnx_w": 1.0 + nrm((L, Dr), 0.1),
        "rwkv_lnx_b": nrm((L, Dr), 0.01),
        "w_rwkv_o": nrm((L, Dr, D), Dr ** -0.5),
        "w_out": nrm((L, D, D), DEEPNORM_BETA * D ** -0.5),
        "ln1_g": 1.0 + nrm((L, D), 0.1),
        "ln1_b": nrm((L, D), 0.01),
        "router_w": nrm((L, D, E), D ** -0.5),
        "router_b": nrm((L, E), 0.01),
        "expert_w1": nrm((L, E, D, 2 * I), D ** -0.5),
        "expert_b1": nrm((L, E, 2 * I), 0.01),
        "expert_w2": nrm((L, E, I, D), DEEPNORM_BETA * I ** -0.5),
        "expert_b2": nrm((L, E, D), 0.01),
        "ln2_g": 1.0 + nrm((L, D), 0.1),
        "ln2_b": nrm((L, D), 0.01),
    }


def reference(x, w_in, attn_sinks, w_attn_o, rwkv_mu, rwkv_w0, rwkv_w2, rwkv_a0, rwkv_a2, rwkv_g2,
              rwkv_k_k, rwkv_k_a, rwkv_r_k, rwkv_v0, rwkv_v1, rwkv_v2, rwkv_lnx_w, rwkv_lnx_b, w_rwkv_o,
              w_out, ln1_g, ln1_b, router_w, router_b, expert_w1, expert_b1, expert_w2, expert_b2,
              ln2_g, ln2_b):
    B, S, D = x.shape
    cos, sin = rope_tables(S, x.dtype)
    v_first = None
    for l in range(DEPTH):
        p = x @ w_in[l]
        q = p[..., :ATT_Q].reshape(B, S, N_Q_HEADS, HEAD_DIM)
        k = p[..., ATT_Q:ATT_Q + ATT_KV].reshape(B, S, N_KV_HEADS, HEAD_DIM)
        v = p[..., ATT_Q + ATT_KV:ATTN_COLS].reshape(B, S, N_KV_HEADS, HEAD_DIM)
        p_rwkv = p[..., ATTN_COLS:ATTN_COLS + RWKV_COLS]
        gate_attn = jax.nn.sigmoid(p[..., ATTN_COLS + RWKV_COLS:ATTN_COLS + RWKV_COLS + D])
        gate_rwkv = jax.nn.sigmoid(p[..., ATTN_COLS + RWKV_COLS + D:])

        y_attn = sliding_window_attention(apply_rope(q, cos, sin), apply_rope(k, cos, sin), v, attn_sinks[l]) @ w_attn_o[l]

        v_res = None if l == 0 else (rwkv_v0[l - 1], rwkv_v1[l - 1], rwkv_v2[l - 1])
        y_rwkv, v_l = rwkv7_time_mix(p_rwkv, rwkv_mu[l], rwkv_w0[l], rwkv_w2[l], rwkv_a0[l], rwkv_a2[l],
                                     rwkv_g2[l], rwkv_k_k[l], rwkv_k_a[l], rwkv_r_k[l], rwkv_lnx_w[l],
                                     rwkv_lnx_b[l], w_rwkv_o[l], v_first, v_res)
        if l == 0:
            v_first = v_l

        mixed = (gate_attn * y_attn + gate_rwkv * y_rwkv) @ w_out[l]
        x = layer_norm(DEEPNORM_ALPHA * x + mixed, ln1_g[l], ln1_b[l])

        ffn = moe_ffn(x, router_w[l], router_b[l], expert_w1[l], expert_b1[l], expert_w2[l], expert_b2[l])
        x = layer_norm(DEEPNORM_ALPHA * x + ffn, ln2_g[l], ln2_b[l])
    return x
```

```python
import functools

import jax
import jax.numpy as jnp
from jax import lax
from jax.experimental import pallas as pl
from jax.experimental.pallas import tpu as pltpu

F32 = jnp.float32
BF16 = jnp.bfloat16

D_MODEL = 1024
HEAD_DIM = 64
N_Q_HEADS = 16
N_KV_HEADS = 4
WINDOW = 128
ATTN_BLOCK = 128
ROPE_THETA = 10000.0
ATT_Q = N_Q_HEADS * HEAD_DIM
ATT_KV = N_KV_HEADS * HEAD_DIM

RWKV_HEAD = 64
RWKV_HEADS = 16
RWKV_DIM = RWKV_HEADS * RWKV_HEAD
DECAY_LORA = 64
AAA_LORA = 64
MV_LORA = 32
GATE_LORA = 128
RWKV_GN_EPS = 64e-5

N_EXPERTS = 32
TOP_K = 4
D_EXPERT = 1024
SWIGLU_LIMIT = 7.0
SWIGLU_ALPHA = 1.702
MOE_BLOCK = 256

LN_EPS = 1e-5

LANES = 128
HEAD_PAIRS = RWKV_HEADS // 2
SCAN_CHUNK = 64
VMEM_LIMIT = 56 * 1024 * 1024

COL_Q, COL_R, COL_K, COL_V, COL_GA, COL_GB = 0, 1024, 2048, 3072, 4096, 5120
COL_AK, COL_AV, COL_LORA = 6144, 6400, 6656
P_COLS = 6912


def _cparams(n_axes):
    return pltpu.CompilerParams(dimension_semantics=("arbitrary",) * n_axes,
                                vmem_limit_bytes=VMEM_LIMIT)


def _dot(a, b):
    return jnp.dot(a, b, preferred_element_type=F32)


def _dot_nt(a, b):
    return lax.dot_general(a, b, (((1,), (1,)), ((), ())), preferred_element_type=F32)


def _dot_tn(a, b):
    return lax.dot_general(a, b, (((0,), (0,)), ((), ())), preferred_element_type=F32)


def _sigmoid(z):
    return 1.0 / (1.0 + jnp.exp(-z))


def _layer_norm(h, g, b):
    mu = jnp.mean(h, axis=-1, keepdims=True)
    c = h - mu
    var = jnp.mean(c * c, axis=-1, keepdims=True)
    return c * lax.rsqrt(var + LN_EPS) * g + b


def _proj_kernel(x_ref, w_ref, o_ref):
    o_ref[...] = _dot(x_ref[...].astype(BF16), w_ref[...]).astype(o_ref.dtype)


def _project(x, w, tm, tn):
    m, k = x.shape
    n = w.shape[1]
    return pl.pallas_call(
        _proj_kernel,
        grid=(n // tn, m // tm),
        in_specs=[pl.BlockSpec((tm, k), lambda j, i: (i, 0)),
                  pl.BlockSpec((k, tn), lambda j, i: (0, j))],
        out_specs=pl.BlockSpec((tm, tn), lambda j, i: (i, j)),
        out_shape=jax.ShapeDtypeStruct((m, n), BF16),
        compiler_params=_cparams(2),
        name="proj",
    )(x, w)


def _rope(x, c, s1, s2):
    return x * c + pltpu.roll(x, LANES - HEAD_DIM // 2, 1) * s1 + pltpu.roll(x, HEAD_DIM // 2, 1) * s2


def _attn_kernel(sink_ref, q_ref, kc_ref, kp_ref, vc_ref, vp_ref, tc_ref, tp_ref, wo_ref, o_ref):
    n = pl.program_id(1)
    blk = ATTN_BLOCK
    lane = lax.broadcasted_iota(jnp.int32, (1, LANES), 1)
    lo = lane < HEAD_DIM
    cq, s1q, s2q = tc_ref[0], tc_ref[1], tc_ref[2]
    cp, s1p, s2p = tp_ref[0], tp_ref[1], tp_ref[2]

    row = lax.broadcasted_iota(jnp.int32, (blk, 2 * blk), 0)
    col = lax.broadcasted_iota(jnp.int32, (blk, 2 * blk), 1)
    valid = (col > row) & (col <= row + WINDOW) & ((n > 0) | (col >= blk))

    def lo_hi(t, off):
        sw = pltpu.roll(t, HEAD_DIM, 1)
        t_lo = jnp.where(lo, t if off == 0 else sw, 0.0).astype(BF16)
        t_hi = jnp.where(lo, 0.0, sw if off == 0 else t).astype(BF16)
        return t_lo, t_hi

    kv_tiles = []
    for c in range(ATT_KV // LANES):
        sl = slice(c * LANES, (c + 1) * LANES)
        kc = _rope(kc_ref[:, sl].astype(F32), cq, s1q, s2q)
        kp = _rope(kp_ref[:, sl].astype(F32), cp, s1p, s2p)
        kt = jnp.concatenate([kp, kc], axis=0)
        vt = jnp.concatenate([vp_ref[:, sl].astype(F32), vc_ref[:, sl].astype(F32)], axis=0)
        kv_tiles.append((kt, vt))

    outs = []
    for g in range(N_KV_HEADS):
        kt, vt = kv_tiles[g // 2]
        k_lo, k_hi = lo_hi(kt, g % 2)
        v_lo, v_hi = lo_hi(vt, g % 2)
        v_st = jnp.concatenate([v_lo, v_hi], axis=0)
        for j in range(2 * g, 2 * g + 2):
            qp = _rope(q_ref[:, j * LANES:(j + 1) * LANES].astype(F32), cq, s1q, s2q)
            qp = (qp * (HEAD_DIM ** -0.5)).astype(BF16)
            probs = []
            for hh, k_m in ((2 * j, k_lo), (2 * j + 1, k_hi)):
                s = jnp.where(valid, _dot_nt(qp, k_m), -1e30)
                sink = sink_ref[hh]
                m = jnp.maximum(jnp.max(s, axis=-1, keepdims=True), sink)
                p = jnp.exp(s - m)
                den = jnp.sum(p, axis=-1, keepdims=True) + jnp.exp(sink - m)
                probs.append((p / den).astype(BF16))
            outs.append(_dot(jnp.concatenate(probs, axis=1), v_st))
    att = jnp.concatenate(outs, axis=1).astype(BF16)
    o_ref[...] = _dot(att, wo_ref[...]).astype(o_ref.dtype)


def _attention(p, sinks, tables, w_o, batch, seq):
    nb = seq // ATTN_BLOCK
    blk = ATTN_BLOCK
    kcol, vcol = COL_AK // ATT_KV, COL_AV // ATT_KV
    cur = lambda b, n: b * nb + n
    prev = lambda b, n: b * nb + jnp.maximum(n - 1, 0)
    return pl.pallas_call(
        _attn_kernel,
        grid=(batch, nb),
        in_specs=[
            pl.BlockSpec(memory_space=pltpu.SMEM),
            pl.BlockSpec((blk, ATT_Q), lambda b, n: (cur(b, n), COL_Q // ATT_Q)),
            pl.BlockSpec((blk, ATT_KV), lambda b, n: (cur(b, n), kcol)),
            pl.BlockSpec((blk, ATT_KV), lambda b, n: (prev(b, n), kcol)),
            pl.BlockSpec((blk, ATT_KV), lambda b, n: (cur(b, n), vcol)),
            pl.BlockSpec((blk, ATT_KV), lambda b, n: (prev(b, n), vcol)),
            pl.BlockSpec((3, blk, LANES), lambda b, n: (0, n, 0)),
            pl.BlockSpec((3, blk, LANES), lambda b, n: (0, jnp.maximum(n - 1, 0), 0)),
            pl.BlockSpec((ATT_Q, D_MODEL), lambda b, n: (0, 0)),
        ],
        out_specs=pl.BlockSpec((blk, D_MODEL), lambda b, n: (cur(b, n), 0)),
        out_shape=jax.ShapeDtypeStruct((batch * seq, D_MODEL), BF16),
        compiler_params=_cparams(2),
        name="attn",
    )(sinks, p, p, p, p, p, tables, tables, w_o)


def _rope_tables(seq):
    pos = jnp.arange(seq, dtype=F32)
    inv = ROPE_THETA ** (-jnp.arange(0, HEAD_DIM, 2, dtype=F32) / HEAD_DIM)
    ang = pos[:, None] * inv[None, :]
    cos, sin = jnp.cos(ang), jnp.sin(ang)
    zero = jnp.zeros_like(sin)
    c = jnp.tile(cos, (1, LANES // (HEAD_DIM // 2)))
    s1 = jnp.tile(jnp.concatenate([-sin, zero], axis=1), (1, LANES // HEAD_DIM))
    s2 = jnp.tile(jnp.concatenate([zero, sin], axis=1), (1, LANES // HEAD_DIM))
    return jnp.stack([c, s1, s2])


def _prep_kernel(has_vres, seq, *refs):
    if has_vres:
        (r_ref, k_ref, v_ref, l_ref, rp_ref, kp_ref, vp_ref, lp_ref, mu_ref, mul_ref, w0_ref, a0_ref,
         w2_ref, a2_ref, g2_ref, vf_ref, v0_ref, v1_ref, v2_ref,
         ro_ref, ko_ref, vo_ref, lw_ref, ao_ref, go_ref) = refs
    else:
        (r_ref, k_ref, v_ref, l_ref, rp_ref, kp_ref, vp_ref, lp_ref, mu_ref, mul_ref, w0_ref, a0_ref,
         w2_ref, a2_ref, g2_ref,
         ro_ref, ko_ref, vo_ref, lw_ref, ao_ref, go_ref) = refs
    tm = r_ref.shape[0]
    i = pl.program_id(0)
    not_first = (lax.rem(i * tm, seq) != 0).astype(F32)
    row0 = lax.broadcasted_iota(jnp.int32, (tm, 1), 0) == 0

    def shift_mix(cur_ref, prev_ref, mu):
        cur = cur_ref[...].astype(F32)
        last = prev_ref[7:8, :].astype(F32) * not_first
        sh = jnp.where(row0, last, pltpu.roll(cur, 1, 0))
        return cur + (sh - cur) * mu

    r = shift_mix(r_ref, rp_ref, mu_ref[0:1, :])
    k = shift_mix(k_ref, kp_ref, mu_ref[1:2, :])
    v = shift_mix(v_ref, vp_ref, mu_ref[2:3, :])
    xl = shift_mix(l_ref, lp_ref, mul_ref[...])
    wa = xl[:, :LANES]
    zw = w0_ref[...] + _dot(jnp.tanh(wa).astype(BF16), w2_ref[...])
    lw_ref[...] = -jnp.exp(-0.5) * _sigmoid(zw)
    a = _sigmoid(a0_ref[...] + _dot(wa.astype(BF16), a2_ref[...]))
    g = _dot(_sigmoid(xl[:, LANES:]).astype(BF16), g2_ref[...])
    if has_vres:
        lo_rank = _dot(v.astype(BF16), v1_ref[...])
        mix = _sigmoid(v0_ref[...] + _dot(lo_rank.astype(BF16), v2_ref[...]))
        v = v + (vf_ref[...].astype(F32) - v) * mix
    ro_ref[...] = r.astype(ro_ref.dtype)
    ko_ref[...] = k.astype(ko_ref.dtype)
    vo_ref[...] = v.astype(vo_ref.dtype)
    ao_ref[...] = a.astype(ao_ref.dtype)
    go_ref[...] = g.astype(go_ref.dtype)


def _rwkv_prep(p, prm, v_first, seq, tm):
    t = p.shape[0]
    dr = RWKV_DIM
    has_vres = v_first is not None
    lw_cols = 2 * LANES
    row = lambda i: (i, 0)
    prev8 = lambda cb: (lambda i: (jnp.maximum(i * (tm // 8) - 1, 0), cb))
    full = lambda a: pl.BlockSpec(a.shape, lambda i: (0,) * a.ndim)
    in_specs = [
        pl.BlockSpec((tm, dr), lambda i: (i, COL_R // dr)),
        pl.BlockSpec((tm, dr), lambda i: (i, COL_K // dr)),
        pl.BlockSpec((tm, dr), lambda i: (i, COL_V // dr)),
        pl.BlockSpec((tm, lw_cols), lambda i: (i, COL_LORA // lw_cols)),
        pl.BlockSpec((8, dr), prev8(COL_R // dr)),
        pl.BlockSpec((8, dr), prev8(COL_K // dr)),
        pl.BlockSpec((8, dr), prev8(COL_V // dr)),
        pl.BlockSpec((8, lw_cols), prev8(COL_LORA // lw_cols)),
    ]
    args = [p] * 8
    small = [prm["mu_rkv"], prm["mu_lora"], prm["w0"], prm["a0"], prm["w2"], prm["a2"], prm["g2"]]
    in_specs += [full(a) for a in small]
    args += small
    if has_vres:
        small_v = [prm["v0"], prm["v1"], prm["v2"]]
        in_specs += [pl.BlockSpec((tm, dr), row)] + [full(a) for a in small_v]
        args += [v_first] + small_v
    out_shape = [jax.ShapeDtypeStruct((t, dr), BF16)] * 3 + [jax.ShapeDtypeStruct((t, dr), F32)] + \
                [jax.ShapeDtypeStruct((t, dr), BF16)] * 2
    return pl.pallas_call(
        functools.partial(_prep_kernel, has_vres, seq),
        grid=(t // tm,),
        in_specs=in_specs,
        out_specs=[pl.BlockSpec((tm, dr), row)] * 6,
        out_shape=out_shape,
        compiler_params=_cparams(1),
        name="rwkv_prep",
    )(*args)


def _scan_kernel(r_ref, k_ref, v_ref, a_ref, g_ref, lw_ref, prm_ref, o_ref, state_ref):
    ch = SCAN_CHUNK
    c_id = pl.program_id(1)

    @pl.when(c_id == 0)
    def _():
        state_ref[...] = jnp.zeros_like(state_ref)

    lw = lw_ref[...]
    tri = (lax.broadcasted_iota(jnp.int32, (ch, ch), 0) >= lax.broadcasted_iota(jnp.int32, (ch, ch), 1)
           ).astype(BF16)
    hi = lw.astype(BF16)
    rem = lw - hi.astype(F32)
    mid = rem.astype(BF16)
    low = (rem - mid.astype(F32)).astype(BF16)
    cum = _dot(tri, hi) + _dot(tri, mid) + _dot(tri, low)
    ref_row = cum[ch // 2 - 1:ch // 2, :]
    total = cum[ch - 1:ch, :]
    e_in = jnp.exp(cum - ref_row)
    e_ex = jnp.exp(cum - lw - ref_row)
    e_inv = jnp.exp(ref_row - cum)
    e_end = jnp.exp(total - cum)
    g_tot = jnp.exp(total)
    e_ref = jnp.exp(ref_row)

    lane = lax.broadcasted_iota(jnp.int32, (1, LANES), 1)
    lo = lane < RWKV_HEAD
    r2 = lax.broadcasted_iota(jnp.int32, (2 * ch, 2 * ch), 0)
    c2 = lax.broadcasted_iota(jnp.int32, (2 * ch, 2 * ch), 1)
    same = ((r2 < ch) & (c2 < ch)) | ((r2 >= ch) & (c2 >= ch))
    tri_strict = same & (r2 > c2)
    tri_incl = same & (r2 >= c2)
    eye = jnp.where(r2 == c2, 1.0, 0.0)
    rb = lax.broadcasted_iota(jnp.int32, (2 * ch, LANES), 0)
    cb = lax.broadcasted_iota(jnp.int32, (2 * ch, LANES), 1)
    blockmask = ((rb < ch) & (cb < RWKV_HEAD)) | ((rb >= ch) & (cb >= RWKV_HEAD))

    def stack(x):
        return jnp.concatenate([jnp.where(lo, x, 0.0), jnp.where(lo, 0.0, x)], axis=0)

    def head_sums(x):
        s_e = jnp.sum(jnp.where(lo, x, 0.0), axis=-1, keepdims=True)
        s_o = jnp.sum(jnp.where(lo, 0.0, x), axis=-1, keepdims=True)
        return jnp.where(lo, s_e, s_o)

    for pr in range(HEAD_PAIRS):
        sl = slice(pr * LANES, (pr + 1) * LANES)
        r = r_ref[:, sl].astype(F32)
        k = k_ref[:, sl].astype(F32)
        v = v_ref[:, sl].astype(F32)
        a = a_ref[:, sl].astype(F32)
        k_k, k_a, r_k = prm_ref[0:1, sl], prm_ref[1:2, sl], prm_ref[2:3, sl]
        lnw, lnb = prm_ref[3:4, sl], prm_ref[4:5, sl]

        kx = k * k_k
        nrm = jnp.sqrt(head_sums(kx * kx))
        kk = kx / jnp.maximum(nrm, 1e-12)
        beta = kk * a
        kmod = k * (1.0 + (a - 1.0) * k_a)

        abar = stack(-kk * e_ex[:, sl]).astype(BF16)
        rbar = stack(r * e_in[:, sl]).astype(BF16)
        bt = stack(beta * e_inv[:, sl]).astype(BF16)
        kt = stack(kmod * e_inv[:, sl]).astype(BF16)
        bh = stack(beta * e_end[:, sl]).astype(BF16)
        kh = stack(kmod * e_end[:, sl]).astype(BF16)
        vs = stack(v).astype(BF16)

        gram = _dot_nt(jnp.concatenate([abar, rbar], axis=0), jnp.concatenate([bt, kt], axis=0))
        a_ab = jnp.where(tri_strict, gram[:2 * ch, :2 * ch], 0.0)
        a_ak = jnp.where(tri_strict, gram[:2 * ch, 2 * ch:], 0.0)
        m_rb = jnp.where(tri_incl, gram[2 * ch:, :2 * ch], 0.0)
        m_rk = jnp.where(tri_incl, gram[2 * ch:, 2 * ch:], 0.0)

        pw = a_ab.astype(BF16)
        tinv = eye + a_ab
        steps = (ch - 1).bit_length() - 1
        for it in range(steps):
            sq = _dot(pw, pw)
            pw = sq.astype(BF16)
            tinv = tinv + _dot(tinv.astype(BF16), pw)

        st = state_ref[pr]
        st_b = (st * e_ref[:, sl]).astype(BF16)
        w = _dot_nt(abar, st_b) + _dot(a_ak.astype(BF16), vs)
        u = _dot(tinv.astype(BF16), w.astype(BF16))
        uv = jnp.concatenate([u.astype(BF16), vs], axis=0)
        o_s = _dot_nt(rbar, st_b) + _dot(jnp.concatenate([m_rb, m_rk], axis=1).astype(BF16), uv)
        state_ref[pr] = st * g_tot[:, sl] + _dot_tn(uv, jnp.concatenate([bh, kh], axis=0))

        mean = jnp.sum(o_s, axis=-1, keepdims=True) * (1.0 / RWKV_HEAD)
        cen = jnp.where(blockmask, o_s - mean, 0.0)
        var = jnp.sum(cen * cen, axis=-1, keepdims=True) * (1.0 / RWKV_HEAD)
        gn_s = cen * lax.rsqrt(var + RWKV_GN_EPS)
        gn = gn_s[:ch] + gn_s[ch:]
        bonus = head_sums(r * kmod * r_k) * v
        z = (gn * lnw + lnb + bonus) * g_ref[:, sl].astype(F32)
        o_ref[:, sl] = z.astype(o_ref.dtype)


def _wkv_scan(r, k, v, a, g, lw, prm, batch, seq):
    ch = SCAN_CHUNK
    nc = seq // ch
    dr = RWKV_DIM
    blk = pl.BlockSpec((ch, dr), lambda b, c: (b * nc + c, 0))
    return pl.pallas_call(
        _scan_kernel,
        grid=(batch, nc),
        in_specs=[blk] * 6 + [pl.BlockSpec(prm.shape, lambda b, c: (0, 0))],
        out_specs=blk,
        out_shape=jax.ShapeDtypeStruct((batch * seq, dr), BF16),
        scratch_shapes=[pltpu.VMEM((HEAD_PAIRS, 2 * RWKV_HEAD, LANES), F32)],
        compiler_params=_cparams(2),
        name="wkv_scan",
    )(r, k, v, a, g, lw, prm)


def _merge_kernel(alpha, z_ref, ya_ref, ga_ref, gb_ref, x_ref, wr_ref, wo_ref, lg_ref, lb_ref, o_ref):
    y_r = _dot(z_ref[...], wr_ref[...])
    mix = _sigmoid(ga_ref[...].astype(F32)) * ya_ref[...].astype(F32) + _sigmoid(gb_ref[...].astype(F32)) * y_r
    mixed = _dot(mix.astype(BF16), wo_ref[...])
    o_ref[...] = _layer_norm(alpha * x_ref[...] + mixed, lg_ref[...], lb_ref[...])


def _merge(z, y_attn, p, x, w_r, w_o, ln_g, ln_b, alpha, tm):
    t, d = x.shape
    row = lambda i: (i, 0)
    full = lambda a: pl.BlockSpec(a.shape, lambda i: (0,) * a.ndim)
    return pl.pallas_call(
        functools.partial(_merge_kernel, alpha),
        grid=(t // tm,),
        in_specs=[pl.BlockSpec((tm, d), row), pl.BlockSpec((tm, d), row),
                  pl.BlockSpec((tm, d), lambda i: (i, COL_GA // d)),
                  pl.BlockSpec((tm, d), lambda i: (i, COL_GB // d)),
                  pl.BlockSpec((tm, d), row), full(w_r), full(w_o), full(ln_g), full(ln_b)],
        out_specs=pl.BlockSpec((tm, d), row),
        out_shape=jax.ShapeDtypeStruct((t, d), F32),
        compiler_params=_cparams(1),
        name="merge_ln1",
    )(z, y_attn, p, p, x, w_r, w_o, ln_g, ln_b)


def _split3(x):
    hi = x.astype(BF16)
    rem = x - hi.astype(F32)
    mid = rem.astype(BF16)
    low = (rem - mid.astype(F32)).astype(BF16)
    return hi, mid, low


def _router_kernel(x_ref, w_ref, b_ref, idx_ref, gate_ref):
    x0, x1, x2 = _split3(x_ref[...])
    w0, w1, w2 = w_ref[0], w_ref[1], w_ref[2]
    logits = (_dot(x0, w0) + (_dot(x0, w1) + _dot(x1, w0)) +
              (_dot(x0, w2) + _dot(x1, w1) + _dot(x2, w0))) + b_ref[...]
    tm = logits.shape[0]
    lane = lax.broadcasted_iota(jnp.int32, (tm, LANES), 1).astype(F32)
    work = logits
    idx_out = jnp.zeros((tm, LANES), F32)
    val_out = jnp.zeros((tm, LANES), F32)
    top = None
    den = jnp.zeros((tm, 1), F32)
    for kk in range(TOP_K):
        m = jnp.max(work, axis=-1, keepdims=True)
        sel = jnp.min(jnp.where(work == m, lane, float(LANES)), axis=-1, keepdims=True)
        if kk == 0:
            top = m
        e = jnp.exp(m - top)
        den = den + e
        idx_out = jnp.where(lane == float(kk), sel, idx_out)
        val_out = jnp.where(lane == float(kk), e, val_out)
        work = jnp.where(lane == sel, -jnp.inf, work)
    idx_ref[...] = idx_out.astype(jnp.int32)
    gate_ref[...] = val_out / den


def _router(x, w3, b, tm):
    t, d = x.shape
    row = lambda i: (i, 0)
    return pl.pallas_call(
        _router_kernel,
        grid=(t // tm,),
        in_specs=[pl.BlockSpec((tm, d), row), pl.BlockSpec(w3.shape, lambda i: (0, 0, 0)),
                  pl.BlockSpec(b.shape, lambda i: (0, 0))],
        out_specs=[pl.BlockSpec((tm, LANES), row)] * 2,
        out_shape=[jax.ShapeDtypeStruct((t, LANES), jnp.int32), jax.ShapeDtypeStruct((t, LANES), F32)],
        compiler_params=_cparams(1),
        name="router",
    )(x, w3, b)


def _row_copy(src_hbm, dst_vmem, sem, src_row, dst_row):
    return pltpu.make_async_copy(src_hbm.at[pl.ds(src_row, 1), :], dst_vmem.at[pl.ds(dst_row, 1), :], sem)


def _gather_kernel(idx_ref, x_hbm, o_ref, sem):
    rows = o_ref.shape[0]

    def issue(r, _):
        _row_copy(x_hbm, o_ref, sem, idx_ref[0, 0, r], r).start()
        return 0

    lax.fori_loop(0, rows, issue, 0)

    def drain(r, _):
        _row_copy(x_hbm, o_ref, sem, 0, r).wait()
        return 0

    lax.fori_loop(0, rows, drain, 0)


def _gather_rows(x, idx, rows_per_step):
    n = idx.shape[0]
    d = x.shape[1]
    steps = n // rows_per_step
    return pl.pallas_call(
        _gather_kernel,
        grid=(steps,),
        in_specs=[pl.BlockSpec((1, 1, rows_per_step), lambda i: (i, 0, 0), memory_space=pltpu.SMEM),
                  pl.BlockSpec(memory_space=pl.ANY)],
        out_specs=pl.BlockSpec((rows_per_step, d), lambda i: (i, 0)),
        out_shape=jax.ShapeDtypeStruct((n, d), x.dtype),
        scratch_shapes=[pltpu.SemaphoreType.DMA(())],
        compiler_params=_cparams(1),
        name="moe_gather",
    )(idx.reshape(steps, 1, rows_per_step), x)


def _expert_kernel(be_ref, nu_ref, x_ref, w1g_ref, w1l_ref, b1g_ref, b1l_ref, w2_ref, b2_ref, o_ref):
    used = pl.program_id(0) < nu_ref[0]

    @pl.when(jnp.logical_not(used))
    def _():
        o_ref[...] = jnp.zeros_like(o_ref)

    @pl.when(used)
    def _():
        xb = x_ref[...].astype(BF16)
        h_glu = _dot(xb, w1g_ref[...]) + b1g_ref[...]
        h_lin = _dot(xb, w1l_ref[...]) + b1l_ref[...]
        x_glu = jnp.minimum(h_glu, SWIGLU_LIMIT)
        x_lin = jnp.clip(h_lin, -SWIGLU_LIMIT, SWIGLU_LIMIT)
        act = x_glu * _sigmoid(SWIGLU_ALPHA * x_glu) * (x_lin + 1.0)
        o_ref[...] = _dot(act.astype(BF16), w2_ref[...]) + b2_ref[...]


def _experts(xg, block_exp, n_used, w1g, w1l, b1g, b1l, w2, b2):
    p_rows, d = xg.shape
    nblk = p_rows // MOE_BLOCK
    de = w1g.shape[2]
    rowmap = lambda i, be, nu: (jnp.minimum(i, nu[0] - 1), 0)
    wmap = lambda i, be, nu: (be[i], 0, 0)
    grid_spec = pltpu.PrefetchScalarGridSpec(
        num_scalar_prefetch=2,
        grid=(nblk,),
        in_specs=[pl.BlockSpec((MOE_BLOCK, d), rowmap),
                  pl.BlockSpec((None, d, de), wmap), pl.BlockSpec((None, d, de), wmap),
                  pl.BlockSpec((None, 1, de), wmap), pl.BlockSpec((None, 1, de), wmap),
                  pl.BlockSpec((None, de, d), wmap), pl.BlockSpec((None, 1, d), wmap)],
        out_specs=pl.BlockSpec((MOE_BLOCK, d), lambda i, be, nu: (i, 0)),
    )
    return pl.pallas_call(
        _expert_kernel,
        grid_spec=grid_spec,
        out_shape=jax.ShapeDtypeStruct((p_rows, d), F32),
        compiler_params=_cparams(1),
        name="moe_experts",
    )(block_exp, n_used, xg, w1g, w1l, b1g, b1l, w2, b2)


def _combine_kernel(alpha, idx_ref, y_hbm, gate_ref, x_ref, lg_ref, lb_ref, o_ref, buf, sem):
    tc = x_ref.shape[0]

    def issue(r, _):
        for kk in range(TOP_K):
            _row_copy(y_hbm, buf.at[kk], sem, idx_ref[0, 0, r * TOP_K + kk], r).start()
        return 0

    lax.fori_loop(0, tc, issue, 0)

    def drain(r, _):
        for kk in range(TOP_K):
            _row_copy(y_hbm, buf.at[kk], sem, 0, r).wait()
        return 0

    lax.fori_loop(0, tc, drain, 0)
    gates = gate_ref[...]
    y = buf[0] * gates[:, 0:1]
    for kk in range(1, TOP_K):
        y = y + buf[kk] * gates[:, kk:kk + 1]
    o_ref[...] = _layer_norm(alpha * x_ref[...] + y, lg_ref[...], lb_ref[...])


def _combine(y_rows, dest, gates, x, ln_g, ln_b, alpha, tc):
    t, d = x.shape
    steps = t // tc
    row = lambda i: (i, 0)
    full = lambda a: pl.BlockSpec(a.shape, lambda i: (0,) * a.ndim)
    return pl.pallas_call(
        functools.partial(_combine_kernel, alpha),
        grid=(steps,),
        in_specs=[pl.BlockSpec((1, 1, tc * TOP_K), lambda i: (i, 0, 0), memory_space=pltpu.SMEM),
                  pl.BlockSpec(memory_space=pl.ANY),
                  pl.BlockSpec((tc, LANES), row), pl.BlockSpec((tc, d), row), full(ln_g), full(ln_b)],
        out_specs=pl.BlockSpec((tc, d), row),
        out_shape=jax.ShapeDtypeStruct((t, d), F32),
        scratch_shapes=[pltpu.VMEM((TOP_K, tc, d), F32), pltpu.SemaphoreType.DMA(())],
        compiler_params=_cparams(1),
        name="moe_combine_ln2",
    )(dest.reshape(steps, 1, tc * TOP_K), y_rows, gates, x, ln_g, ln_b)


def _routing_tables(top_idx):
    t = top_idx.shape[0]
    n_assign = t * TOP_K
    e_flat = top_idx.reshape(n_assign)
    onehot = (e_flat[:, None] == jnp.arange(N_EXPERTS, dtype=jnp.int32)[None, :]).astype(jnp.int32)
    incl = jnp.cumsum(onehot, axis=0)
    counts = incl[-1]
    rank = jnp.sum((incl - onehot) * onehot, axis=1)
    padded = ((counts + MOE_BLOCK - 1) // MOE_BLOCK) * MOE_BLOCK
    pend = jnp.cumsum(padded)
    pstart = pend - padded
    dest = (pstart[e_flat] + rank).astype(jnp.int32)
    n_blocks = -(-n_assign // MOE_BLOCK) + N_EXPERTS
    tok_flat = jnp.arange(n_assign, dtype=jnp.int32) // TOP_K
    row_tok = jnp.zeros((n_blocks * MOE_BLOCK,), jnp.int32).at[dest].set(tok_flat)
    block_start = jnp.arange(n_blocks, dtype=jnp.int32) * MOE_BLOCK
    block_exp = jnp.minimum(jnp.searchsorted(pend, block_start, side="right"), N_EXPERTS - 1).astype(jnp.int32)
    n_used = (pend[-1] // MOE_BLOCK).astype(jnp.int32).reshape(1)
    return dest, row_tok, block_exp, n_used


def _pad_rows(w, rows, at):
    out = jnp.zeros((rows, w.shape[1]), w.dtype)
    return out.at[at:at + w.shape[0]].set(w)


def _tile(n, pref):
    return pref if n % pref == 0 else n


def kernel(x, w_in, attn_sinks, w_attn_o, rwkv_mu, rwkv_w0, rwkv_w2, rwkv_a0, rwkv_a2, rwkv_g2, rwkv_k_k, rwkv_k_a, rwkv_r_k, rwkv_v0, rwkv_v1, rwkv_v2, rwkv_lnx_w, rwkv_lnx_b, w_rwkv_o, w_out, ln1_g, ln1_b, router_w, router_b, expert_w1, expert_b1, expert_w2, expert_b2, ln2_g, ln2_b):
    batch, seq, d = x.shape
    depth = w_in.shape[0]
    t = batch * seq
    alpha = (2.0 * depth) ** 0.25
    dr = RWKV_DIM
    tables = _rope_tables(seq)
    xf = x.reshape(t, d)
    v_first = None
    o_att, o_rw = ATT_Q + 2 * ATT_KV, ATT_Q + 2 * ATT_KV + 3 * dr + DECAY_LORA + AAA_LORA + GATE_LORA

    for l in range(depth):
        wl = w_in[l]
        wp = jnp.concatenate([
            wl[:, :ATT_Q], wl[:, o_att:o_att + 3 * dr], wl[:, o_rw:o_rw + 2 * d],
            wl[:, ATT_Q:o_att], wl[:, o_att + 3 * dr:o_rw]], axis=1).astype(BF16)
        p = _project(xf, wp, _tile(t, 512), P_COLS // 3)

        y_attn = _attention(p, attn_sinks[l], tables, w_attn_o[l].astype(BF16), batch, seq)

        mu = rwkv_mu[l]
        prm = {
            "mu_rkv": mu[:3 * dr].reshape(3, dr),
            "mu_lora": mu[3 * dr:].reshape(1, -1),
            "w0": rwkv_w0[l].reshape(1, dr), "a0": rwkv_a0[l].reshape(1, dr),
            "w2": _pad_rows(rwkv_w2[l], LANES, 0).astype(BF16),
            "a2": _pad_rows(rwkv_a2[l], LANES, DECAY_LORA).astype(BF16),
            "g2": rwkv_g2[l].astype(BF16),
        }
        if l > 0:
            prm["v0"] = rwkv_v0[l - 1].reshape(1, dr)
            prm["v1"] = jnp.zeros((dr, LANES), F32).at[:, :MV_LORA].set(rwkv_v1[l - 1]).astype(BF16)
            prm["v2"] = _pad_rows(rwkv_v2[l - 1], LANES, 0).astype(BF16)
        r_, k_, v_, lw_, a_, g_ = _rwkv_prep(p, prm, v_first, seq, _tile(seq, 256))
        if l == 0:
            v_first = v_
        scan_prm = jnp.stack([rwkv_k_k[l], rwkv_k_a[l], rwkv_r_k[l].reshape(dr), rwkv_lnx_w[l], rwkv_lnx_b[l],
                              jnp.zeros((dr,), F32), jnp.zeros((dr,), F32), jnp.zeros((dr,), F32)])
        z = _wkv_scan(r_, k_, v_, a_, g_, lw_, scan_prm, batch, seq)

        x1 = _merge(z, y_attn, p, xf, w_rwkv_o[l].astype(BF16), w_out[l].astype(BF16),
                    ln1_g[l].reshape(1, d), ln1_b[l].reshape(1, d), alpha, _tile(t, 256))

        rw = jnp.zeros((d, LANES), F32).at[:, :N_EXPERTS].set(router_w[l])
        rb = jnp.full((1, LANES), -1e30, F32).at[0, :N_EXPERTS].set(router_b[l])
        top_idx, gates = _router(x1, jnp.stack(_split3(rw)), rb, _tile(t, 512))
        dest, row_tok, block_exp, n_used = _routing_tables(top_idx[:, :TOP_K])

        xg = _gather_rows(x1, row_tok, MOE_BLOCK)
        w1 = expert_w1[l]
        b1 = expert_b1[l]
        y_rows = _experts(xg, block_exp, n_used,
                          w1[:, :, 0::2].astype(BF16), w1[:, :, 1::2].astype(BF16),
                          b1[:, None, 0::2], b1[:, None, 1::2],
                          expert_w2[l].astype(BF16), expert_b2[l][:, None, :])
        xf = _combine(y_rows, dest, gates, x1, ln2_g[l].reshape(1, d), ln2_b[l].reshape(1, d),
                      alpha, _tile(t, 128))
    return xf.reshape(batch, seq, d)
```

```python
import functools

import jax
import jax.numpy as jnp
from jax import lax
from jax.experimental import pallas as pl
from jax.experimental.pallas import tpu as pltpu

F32 = jnp.float32
BF16 = jnp.bfloat16

D_MODEL = 1024
HEAD_DIM = 64
N_Q_HEADS = 16
N_KV_HEADS = 4
WINDOW = 128
ATTN_BLOCK = 128
ROPE_THETA = 10000.0
ATT_Q = N_Q_HEADS * HEAD_DIM
ATT_KV = N_KV_HEADS * HEAD_DIM

RWKV_HEAD = 64
RWKV_HEADS = 16
RWKV_DIM = RWKV_HEADS * RWKV_HEAD
DECAY_LORA = 64
AAA_LORA = 64
MV_LORA = 32
GATE_LORA = 128
RWKV_GN_EPS = 64e-5

N_EXPERTS = 32
TOP_K = 4
D_EXPERT = 1024
SWIGLU_LIMIT = 7.0
SWIGLU_ALPHA = 1.702
MOE_BLOCK = 256

LN_EPS = 1e-5

LANES = 128
HEAD_PAIRS = RWKV_HEADS // 2
SCAN_CHUNK = 64
VMEM_LIMIT = 56 * 1024 * 1024

COL_Q, COL_R, COL_K, COL_V, COL_GA, COL_GB = 0, 1024, 2048, 3072, 4096, 5120
COL_AK, COL_AV, COL_LORA = 6144, 6400, 6656
P_COLS = 6912


def _cparams(n_axes):
    return pltpu.CompilerParams(dimension_semantics=("arbitrary",) * n_axes,
                                vmem_limit_bytes=VMEM_LIMIT)


def _dot(a, b):
    return jnp.dot(a, b, preferred_element_type=F32)


def _dot_nt(a, b):
    return lax.dot_general(a, b, (((1,), (1,)), ((), ())), preferred_element_type=F32)


def _dot_tn(a, b):
    return lax.dot_general(a, b, (((0,), (0,)), ((), ())), preferred_element_type=F32)


def _sigmoid(z):
    return 1.0 / (1.0 + jnp.exp(-z))


def _layer_norm(h, g, b):
    mu = jnp.mean(h, axis=-1, keepdims=True)
    c = h - mu
    var = jnp.mean(c * c, axis=-1, keepdims=True)
    return c * lax.rsqrt(var + LN_EPS) * g + b


def _proj_kernel(x_ref, w_ref, o_ref):
    o_ref[...] = _dot(x_ref[...].astype(BF16), w_ref[...]).astype(o_ref.dtype)


def _project(x, w, tm, tn):
    m, k = x.shape
    n = w.shape[1]
    return pl.pallas_call(
        _proj_kernel,
        grid=(n // tn, m // tm),
        in_specs=[pl.BlockSpec((tm, k), lambda j, i: (i, 0)),
                  pl.BlockSpec((k, tn), lambda j, i: (0, j))],
        out_specs=pl.BlockSpec((tm, tn), lambda j, i: (i, j)),
        out_shape=jax.ShapeDtypeStruct((m, n), BF16),
        compiler_params=_cparams(2),
        name="proj",
    )(x, w)


def _rope(x, c, s1, s2):
    return x * c + pltpu.roll(x, LANES - HEAD_DIM // 2, 1) * s1 + pltpu.roll(x, HEAD_DIM // 2, 1) * s2


def _attn_kernel(sink_ref, q_ref, kc_ref, kp_ref, vc_ref, vp_ref, tc_ref, tp_ref, wo_ref, o_ref):
    n = pl.program_id(1)
    blk = ATTN_BLOCK
    lane = lax.broadcasted_iota(jnp.int32, (1, LANES), 1)
    lo = lane < HEAD_DIM
    cq, s1q, s2q = tc_ref[0], tc_ref[1], tc_ref[2]
    cp, s1p, s2p = tp_ref[0], tp_ref[1], tp_ref[2]

    row = lax.broadcasted_iota(jnp.int32, (blk, 2 * blk), 0)
    col = lax.broadcasted_iota(jnp.int32, (blk, 2 * blk), 1)
    valid = (col > row) & (col <= row + WINDOW) & ((n > 0) | (col >= blk))

    def lo_hi(t, off):
        sw = pltpu.roll(t, HEAD_DIM, 1)
        t_lo = jnp.where(lo, t if off == 0 else sw, 0.0).astype(BF16)
        t_hi = jnp.where(lo, 0.0, sw if off == 0 else t).astype(BF16)
        return t_lo, t_hi

    kv_tiles = []
    for c in range(ATT_KV // LANES):
        sl = slice(c * LANES, (c + 1) * LANES)
        kc = _rope(kc_ref[:, sl].astype(F32), cq, s1q, s2q)
        kp = _rope(kp_ref[:, sl].astype(F32), cp, s1p, s2p)
        kt = jnp.concatenate([kp, kc], axis=0)
        vt = jnp.concatenate([vp_ref[:, sl].astype(F32), vc_ref[:, sl].astype(F32)], axis=0)
        kv_tiles.append((kt, vt))

    k_m, v_st = [], []
    for g in range(N_KV_HEADS):
        kt, vt = kv_tiles[g // 2]
        k_m.append(lo_hi(kt, g % 2))
        v_st.append(jnp.concatenate(lo_hi(vt, g % 2), axis=0))
    n_pairs = N_Q_HEADS // 2
    qp = [(_rope(q_ref[:, j * LANES:(j + 1) * LANES].astype(F32), cq, s1q, s2q) * (HEAD_DIM ** -0.5)).astype(BF16)
          for j in range(n_pairs)]
    scores = [_dot_nt(qp[hh // 2], k_m[hh // 4][hh % 2]) for hh in range(N_Q_HEADS)]
    probs = []
    for hh in range(N_Q_HEADS):
        s = jnp.where(valid, scores[hh], -1e30)
        sink = sink_ref[hh]
        m = jnp.maximum(jnp.max(s, axis=-1, keepdims=True), sink)
        p = jnp.exp(s - m)
        den = jnp.sum(p, axis=-1, keepdims=True) + jnp.exp(sink - m)
        probs.append((p / den).astype(BF16))
    outs = [_dot(jnp.concatenate([probs[2 * j], probs[2 * j + 1]], axis=1), v_st[j // 2]) for j in range(n_pairs)]
    att = jnp.concatenate(outs, axis=1).astype(BF16)
    o_ref[...] = _dot(att, wo_ref[...]).astype(o_ref.dtype)


def _attention(p, sinks, tables, w_o, batch, seq):
    nb = seq // ATTN_BLOCK
    blk = ATTN_BLOCK
    kcol, vcol = COL_AK // ATT_KV, COL_AV // ATT_KV
    cur = lambda b, n: b * nb + n
    prev = lambda b, n: b * nb + jnp.maximum(n - 1, 0)
    return pl.pallas_call(
        _attn_kernel,
        grid=(batch, nb),
        in_specs=[
            pl.BlockSpec(memory_space=pltpu.SMEM),
            pl.BlockSpec((blk, ATT_Q), lambda b, n: (cur(b, n), COL_Q // ATT_Q)),
            pl.BlockSpec((blk, ATT_KV), lambda b, n: (cur(b, n), kcol)),
            pl.BlockSpec((blk, ATT_KV), lambda b, n: (prev(b, n), kcol)),
            pl.BlockSpec((blk, ATT_KV), lambda b, n: (cur(b, n), vcol)),
            pl.BlockSpec((blk, ATT_KV), lambda b, n: (prev(b, n), vcol)),
            pl.BlockSpec((3, blk, LANES), lambda b, n: (0, n, 0)),
            pl.BlockSpec((3, blk, LANES), lambda b, n: (0, jnp.maximum(n - 1, 0), 0)),
            pl.BlockSpec((ATT_Q, D_MODEL), lambda b, n: (0, 0)),
        ],
        out_specs=pl.BlockSpec((blk, D_MODEL), lambda b, n: (cur(b, n), 0)),
        out_shape=jax.ShapeDtypeStruct((batch * seq, D_MODEL), BF16),
        compiler_params=_cparams(2),
        name="attn",
    )(sinks, p, p, p, p, p, tables, tables, w_o)


def _rope_tables(seq):
    pos = jnp.arange(seq, dtype=F32)
    inv = ROPE_THETA ** (-jnp.arange(0, HEAD_DIM, 2, dtype=F32) / HEAD_DIM)
    ang = pos[:, None] * inv[None, :]
    cos, sin = jnp.cos(ang), jnp.sin(ang)
    zero = jnp.zeros_like(sin)
    c = jnp.tile(cos, (1, LANES // (HEAD_DIM // 2)))
    s1 = jnp.tile(jnp.concatenate([-sin, zero], axis=1), (1, LANES // HEAD_DIM))
    s2 = jnp.tile(jnp.concatenate([zero, sin], axis=1), (1, LANES // HEAD_DIM))
    return jnp.stack([c, s1, s2])


def _prep_kernel(has_vres, seq, *refs):
    if has_vres:
        (r_ref, k_ref, v_ref, l_ref, rp_ref, kp_ref, vp_ref, lp_ref, mu_ref, mul_ref, w0_ref, a0_ref,
         w2_ref, a2_ref, g2_ref, vf_ref, v0_ref, v1_ref, v2_ref,
         ro_ref, ko_ref, vo_ref, lw_ref, ao_ref, go_ref) = refs
    else:
        (r_ref, k_ref, v_ref, l_ref, rp_ref, kp_ref, vp_ref, lp_ref, mu_ref, mul_ref, w0_ref, a0_ref,
         w2_ref, a2_ref, g2_ref,
         ro_ref, ko_ref, vo_ref, lw_ref, ao_ref, go_ref) = refs
    tm = r_ref.shape[0]
    i = pl.program_id(0)
    not_first = (lax.rem(i * tm, seq) != 0).astype(F32)
    row0 = lax.broadcasted_iota(jnp.int32, (tm, 1), 0) == 0

    def shift_mix(cur_ref, prev_ref, mu):
        cur = cur_ref[...].astype(F32)
        last = prev_ref[7:8, :].astype(F32) * not_first
        sh = jnp.where(row0, last, pltpu.roll(cur, 1, 0))
        return cur + (sh - cur) * mu

    r = shift_mix(r_ref, rp_ref, mu_ref[0:1, :])
    k = shift_mix(k_ref, kp_ref, mu_ref[1:2, :])
    v = shift_mix(v_ref, vp_ref, mu_ref[2:3, :])
    xl = shift_mix(l_ref, lp_ref, mul_ref[...])
    wa = xl[:, :LANES]
    zw = w0_ref[...] + _dot(jnp.tanh(wa).astype(BF16), w2_ref[...])
    lw_ref[...] = -jnp.exp(-0.5) * _sigmoid(zw)
    a = _sigmoid(a0_ref[...] + _dot(wa.astype(BF16), a2_ref[...]))
    g = _dot(_sigmoid(xl[:, LANES:]).astype(BF16), g2_ref[...])
    if has_vres:
        lo_rank = _dot(v.astype(BF16), v1_ref[...])
        mix = _sigmoid(v0_ref[...] + _dot(lo_rank.astype(BF16), v2_ref[...]))
        v = v + (vf_ref[...].astype(F32) - v) * mix
    ro_ref[...] = r.astype(ro_ref.dtype)
    ko_ref[...] = k.astype(ko_ref.dtype)
    vo_ref[...] = v.astype(vo_ref.dtype)
    ao_ref[...] = a.astype(ao_ref.dtype)
    go_ref[...] = g.astype(go_ref.dtype)


def _rwkv_prep(p, prm, v_first, seq, tm):
    t = p.shape[0]
    dr = RWKV_DIM
    has_vres = v_first is not None
    lw_cols = 2 * LANES
    row = lambda i: (i, 0)
    prev8 = lambda cb: (lambda i: (jnp.maximum(i * (tm // 8) - 1, 0), cb))
    full = lambda a: pl.BlockSpec(a.shape, lambda i: (0,) * a.ndim)
    in_specs = [
        pl.BlockSpec((tm, dr), lambda i: (i, COL_R // dr)),
        pl.BlockSpec((tm, dr), lambda i: (i, COL_K // dr)),
        pl.BlockSpec((tm, dr), lambda i: (i, COL_V // dr)),
        pl.BlockSpec((tm, lw_cols), lambda i: (i, COL_LORA // lw_cols)),
        pl.BlockSpec((8, dr), prev8(COL_R // dr)),
        pl.BlockSpec((8, dr), prev8(COL_K // dr)),
        pl.BlockSpec((8, dr), prev8(COL_V // dr)),
        pl.BlockSpec((8, lw_cols), prev8(COL_LORA // lw_cols)),
    ]
    args = [p] * 8
    small = [prm["mu_rkv"], prm["mu_lora"], prm["w0"], prm["a0"], prm["w2"], prm["a2"], prm["g2"]]
    in_specs += [full(a) for a in small]
    args += small
    if has_vres:
        small_v = [prm["v0"], prm["v1"], prm["v2"]]
        in_specs += [pl.BlockSpec((tm, dr), row)] + [full(a) for a in small_v]
        args += [v_first] + small_v
    out_shape = [jax.ShapeDtypeStruct((t, dr), BF16)] * 3 + [jax.ShapeDtypeStruct((t, dr), F32)] + \
                [jax.ShapeDtypeStruct((t, dr), BF16)] * 2
    return pl.pallas_call(
        functools.partial(_prep_kernel, has_vres, seq),
        grid=(t // tm,),
        in_specs=in_specs,
        out_specs=[pl.BlockSpec((tm, dr), row)] * 6,
        out_shape=out_shape,
        compiler_params=_cparams(1),
        name="rwkv_prep",
    )(*args)


def _scan_kernel(r_ref, k_ref, v_ref, a_ref, g_ref, lw_ref, prm_ref, o_ref, state_ref):
    ch = SCAN_CHUNK
    c_id = pl.program_id(1)

    @pl.when(c_id == 0)
    def _():
        state_ref[...] = jnp.zeros_like(state_ref)

    lw = lw_ref[...]
    tri = (lax.broadcasted_iota(jnp.int32, (ch, ch), 0) >= lax.broadcasted_iota(jnp.int32, (ch, ch), 1)
           ).astype(BF16)
    hi = lw.astype(BF16)
    rem = lw - hi.astype(F32)
    mid = rem.astype(BF16)
    low = (rem - mid.astype(F32)).astype(BF16)
    cum = _dot(tri, hi) + _dot(tri, mid) + _dot(tri, low)
    ref_row = cum[ch // 2 - 1:ch // 2, :]
    total = cum[ch - 1:ch, :]
    e_in = jnp.exp(cum - ref_row)
    e_ex = jnp.exp(cum - lw - ref_row)
    e_inv = jnp.exp(ref_row - cum)
    e_end = jnp.exp(total - cum)
    g_tot = jnp.exp(total)
    e_ref = jnp.exp(ref_row)

    lane = lax.broadcasted_iota(jnp.int32, (1, LANES), 1)
    lo = lane < RWKV_HEAD
    r2 = lax.broadcasted_iota(jnp.int32, (2 * ch, 2 * ch), 0)
    c2 = lax.broadcasted_iota(jnp.int32, (2 * ch, 2 * ch), 1)
    same = ((r2 < ch) & (c2 < ch)) | ((r2 >= ch) & (c2 >= ch))
    tri_strict = same & (r2 > c2)
    tri_incl = same & (r2 >= c2)
    eye = jnp.where(r2 == c2, 1.0, 0.0)
    rb = lax.broadcasted_iota(jnp.int32, (2 * ch, LANES), 0)
    cb = lax.broadcasted_iota(jnp.int32, (2 * ch, LANES), 1)
    blockmask = ((rb < ch) & (cb < RWKV_HEAD)) | ((rb >= ch) & (cb >= RWKV_HEAD))

    def stack(x):
        return jnp.concatenate([jnp.where(lo, x, 0.0), jnp.where(lo, 0.0, x)], axis=0)

    def head_sums(x):
        s_e = jnp.sum(jnp.where(lo, x, 0.0), axis=-1, keepdims=True)
        s_o = jnp.sum(jnp.where(lo, 0.0, x), axis=-1, keepdims=True)
        return jnp.where(lo, s_e, s_o)

    pairs = range(HEAD_PAIRS)
    sls = [slice(pr * LANES, (pr + 1) * LANES) for pr in pairs]

    def kmod_of(sl):
        return k_ref[:, sl].astype(F32) * (1.0 + (a_ref[:, sl].astype(F32) - 1.0) * prm_ref[1:2, sl])

    abar, rbar, vs, lhs, rhs, upd = [], [], [], [], [], []
    for sl in sls:
        a = a_ref[:, sl].astype(F32)
        kx = k_ref[:, sl].astype(F32) * prm_ref[0:1, sl]
        kk = kx / jnp.maximum(jnp.sqrt(head_sums(kx * kx)), 1e-12)
        beta = kk * a
        kmod = kmod_of(sl)
        ab = stack(-kk * e_ex[:, sl]).astype(BF16)
        rb_ = stack(r_ref[:, sl].astype(F32) * e_in[:, sl]).astype(BF16)
        abar.append(ab)
        rbar.append(rb_)
        lhs.append(jnp.concatenate([ab, rb_], axis=0))
        rhs.append(jnp.concatenate([stack(beta * e_inv[:, sl]).astype(BF16),
                                    stack(kmod * e_inv[:, sl]).astype(BF16)], axis=0))
        upd.append(jnp.concatenate([stack(beta * e_end[:, sl]).astype(BF16),
                                    stack(kmod * e_end[:, sl]).astype(BF16)], axis=0))
        vs.append(stack(v_ref[:, sl].astype(F32)).astype(BF16))

    grams = [_dot_nt(lhs[pr], rhs[pr]) for pr in pairs]
    a_ak = [jnp.where(tri_strict, gm[:2 * ch, 2 * ch:], 0.0).astype(BF16) for gm in grams]
    m_r = [jnp.concatenate([jnp.where(tri_incl, gm[2 * ch:, :2 * ch], 0.0),
                            jnp.where(tri_incl, gm[2 * ch:, 2 * ch:], 0.0)], axis=1).astype(BF16) for gm in grams]

    a_ab = [jnp.where(tri_strict, gm[:2 * ch, :2 * ch], 0.0) for gm in grams]
    pw = [x.astype(BF16) for x in a_ab]
    tinv = [eye + x for x in a_ab]
    for _ in range((ch - 1).bit_length() - 1):
        pw = [_dot(x, x).astype(BF16) for x in pw]
        tinv = [tinv[pr] + _dot(tinv[pr].astype(BF16), pw[pr]) for pr in pairs]

    st = [state_ref[pr] for pr in pairs]
    st_b = [(st[pr] * e_ref[:, sls[pr]]).astype(BF16) for pr in pairs]
    w = [_dot_nt(abar[pr], st_b[pr]) + _dot(a_ak[pr], vs[pr]) for pr in pairs]
    u = [_dot(tinv[pr].astype(BF16), w[pr].astype(BF16)) for pr in pairs]
    uv = [jnp.concatenate([u[pr].astype(BF16), vs[pr]], axis=0) for pr in pairs]
    o_s = [_dot_nt(rbar[pr], st_b[pr]) + _dot(m_r[pr], uv[pr]) for pr in pairs]
    for pr in pairs:
        state_ref[pr] = st[pr] * g_tot[:, sls[pr]] + _dot_tn(uv[pr], upd[pr])

    for pr, sl in enumerate(sls):
        mean = jnp.sum(o_s[pr], axis=-1, keepdims=True) * (1.0 / RWKV_HEAD)
        cen = jnp.where(blockmask, o_s[pr] - mean, 0.0)
        var = jnp.sum(cen * cen, axis=-1, keepdims=True) * (1.0 / RWKV_HEAD)
        gn_s = cen * lax.rsqrt(var + RWKV_GN_EPS)
        gn = gn_s[:ch] + gn_s[ch:]
        v = v_ref[:, sl].astype(F32)
        bonus = head_sums(r_ref[:, sl].astype(F32) * kmod_of(sl) * prm_ref[2:3, sl]) * v
        z = (gn * prm_ref[3:4, sl] + prm_ref[4:5, sl] + bonus) * g_ref[:, sl].astype(F32)
        o_ref[:, sl] = z.astype(o_ref.dtype)


def _wkv_scan(r, k, v, a, g, lw, prm, batch, seq):
    ch = SCAN_CHUNK
    nc = seq // ch
    dr = RWKV_DIM
    blk = pl.BlockSpec((ch, dr), lambda b, c: (b * nc + c, 0))
    return pl.pallas_call(
        _scan_kernel,
        grid=(batch, nc),
        in_specs=[blk] * 6 + [pl.BlockSpec(prm.shape, lambda b, c: (0, 0))],
        out_specs=blk,
        out_shape=jax.ShapeDtypeStruct((batch * seq, dr), BF16),
        scratch_shapes=[pltpu.VMEM((HEAD_PAIRS, 2 * RWKV_HEAD, LANES), F32)],
        compiler_params=_cparams(2),
        name="wkv_scan",
    )(r, k, v, a, g, lw, prm)


def _merge_kernel(alpha, z_ref, ya_ref, ga_ref, gb_ref, x_ref, wr_ref, wo_ref, lg_ref, lb_ref, o_ref):
    y_r = _dot(z_ref[...], wr_ref[...])
    mix = _sigmoid(ga_ref[...].astype(F32)) * ya_ref[...].astype(F32) + _sigmoid(gb_ref[...].astype(F32)) * y_r
    mixed = _dot(mix.astype(BF16), wo_ref[...])
    o_ref[...] = _layer_norm(alpha * x_ref[...] + mixed, lg_ref[...], lb_ref[...])


def _merge(z, y_attn, p, x, w_r, w_o, ln_g, ln_b, alpha, tm):
    t, d = x.shape
    row = lambda i: (i, 0)
    full = lambda a: pl.BlockSpec(a.shape, lambda i: (0,) * a.ndim)
    return pl.pallas_call(
        functools.partial(_merge_kernel, alpha),
        grid=(t // tm,),
        in_specs=[pl.BlockSpec((tm, d), row), pl.BlockSpec((tm, d), row),
                  pl.BlockSpec((tm, d), lambda i: (i, COL_GA // d)),
                  pl.BlockSpec((tm, d), lambda i: (i, COL_GB // d)),
                  pl.BlockSpec((tm, d), row), full(w_r), full(w_o), full(ln_g), full(ln_b)],
        out_specs=pl.BlockSpec((tm, d), row),
        out_shape=jax.ShapeDtypeStruct((t, d), F32),
        compiler_params=_cparams(1),
        name="merge_ln1",
    )(z, y_attn, p, p, x, w_r, w_o, ln_g, ln_b)


def _split3(x):
    hi = x.astype(BF16)
    rem = x - hi.astype(F32)
    mid = rem.astype(BF16)
    low = (rem - mid.astype(F32)).astype(BF16)
    return hi, mid, low


def _router_kernel(x_ref, w_ref, b_ref, idx_ref, gate_ref, rank_ref, cnt_ref, carry_ref):
    @pl.when(pl.program_id(0) == 0)
    def _():
        carry_ref[...] = jnp.zeros_like(carry_ref)

    x0, x1, x2 = _split3(x_ref[...])
    w0, w1, w2 = w_ref[0], w_ref[1], w_ref[2]
    logits = (_dot(x0, w0) + (_dot(x0, w1) + _dot(x1, w0)) +
              (_dot(x0, w2) + _dot(x1, w1) + _dot(x2, w0))) + b_ref[...]
    tm = logits.shape[0]
    lane = lax.broadcasted_iota(jnp.int32, (tm, LANES), 1).astype(F32)
    work = logits
    idx_out = jnp.zeros((tm, LANES), F32)
    val_out = jnp.zeros((tm, LANES), F32)
    top = None
    den = jnp.zeros((tm, 1), F32)
    hits = []
    for kk in range(TOP_K):
        m = jnp.max(work, axis=-1, keepdims=True)
        sel = jnp.min(jnp.where(work == m, lane, float(LANES)), axis=-1, keepdims=True)
        if kk == 0:
            top = m
        e = jnp.exp(m - top)
        den = den + e
        idx_out = jnp.where(lane == float(kk), sel, idx_out)
        val_out = jnp.where(lane == float(kk), e, val_out)
        hits.append(lane == sel)
        work = jnp.where(hits[-1], -jnp.inf, work)
    idx_ref[...] = idx_out.astype(jnp.int32)
    gate_ref[...] = val_out / den

    chosen = jnp.where(hits[0] | hits[1] | hits[2] | hits[3], 1.0, 0.0)
    r_i = lax.broadcasted_iota(jnp.int32, (tm, tm), 0)
    c_i = lax.broadcasted_iota(jnp.int32, (tm, tm), 1)
    earlier = jnp.where(r_i > c_i, 1.0, 0.0).astype(BF16)
    before = carry_ref[0:1, :] + _dot(earlier, chosen.astype(BF16))
    rank_out = jnp.zeros((tm, LANES), F32)
    for kk in range(TOP_K):
        rk = jnp.sum(jnp.where(hits[kk], before, 0.0), axis=-1, keepdims=True)
        rank_out = jnp.where(lane == float(kk), rk, rank_out)
    rank_ref[...] = rank_out.astype(jnp.int32)
    carry_ref[0:1, :] = carry_ref[0:1, :] + jnp.sum(chosen, axis=0, keepdims=True)
    cnt_ref[...] = carry_ref[...].astype(jnp.int32)


def _router(x, w3, b, tm):
    t, d = x.shape
    row = lambda i: (i, 0)
    return pl.pallas_call(
        _router_kernel,
        grid=(t // tm,),
        in_specs=[pl.BlockSpec((tm, d), row), pl.BlockSpec(w3.shape, lambda i: (0, 0, 0)),
                  pl.BlockSpec(b.shape, lambda i: (0, 0))],
        out_specs=[pl.BlockSpec((tm, LANES), row)] * 3 + [pl.BlockSpec((8, LANES), lambda i: (0, 0))],
        out_shape=[jax.ShapeDtypeStruct((t, LANES), jnp.int32), jax.ShapeDtypeStruct((t, LANES), F32),
                   jax.ShapeDtypeStruct((t, LANES), jnp.int32), jax.ShapeDtypeStruct((8, LANES), jnp.int32)],
        scratch_shapes=[pltpu.VMEM((8, LANES), F32)],
        compiler_params=_cparams(1),
        name="router",
    )(x, w3, b)


def _row_copy(src_hbm, dst_vmem, sem, src_row, dst_row):
    return pltpu.make_async_copy(src_hbm.at[pl.ds(src_row, 1), :], dst_vmem.at[pl.ds(dst_row, 1), :], sem)


DEINT_GROUP = 2 * LANES


def _deint_kernel(w_ref, sel_ref, g_ref, l_ref):
    for c in range(w_ref.shape[1] // DEINT_GROUP):
        blk = w_ref[:, c * DEINT_GROUP:(c + 1) * DEINT_GROUP].astype(BF16)
        both = _dot(blk, sel_ref[...])
        g_ref[:, c * LANES:(c + 1) * LANES] = both[:, :LANES].astype(BF16)
        l_ref[:, c * LANES:(c + 1) * LANES] = both[:, LANES:].astype(BF16)


def _deinterleave(w, tm):
    rows, cols = w.shape
    src = jnp.arange(DEINT_GROUP)[:, None]
    dst = jnp.arange(DEINT_GROUP)[None, :]
    sel = jnp.where(dst < LANES, src == 2 * dst, src == 2 * (dst - LANES) + 1).astype(BF16)
    half = jax.ShapeDtypeStruct((rows, cols // 2), BF16)
    return pl.pallas_call(
        _deint_kernel,
        grid=(rows // tm,),
        in_specs=[pl.BlockSpec((tm, cols), lambda i: (i, 0)), pl.BlockSpec(sel.shape, lambda i: (0, 0))],
        out_specs=[pl.BlockSpec((tm, cols // 2), lambda i: (i, 0))] * 2,
        out_shape=[half, half],
        compiler_params=_cparams(1),
        name="deinterleave",
    )(w, sel)


def _expert_kernel(be_ref, nu_ref, idx0_ref, idxn_ref, x_hbm, w1g_ref, w1l_ref, b1g_ref, b1l_ref, w2_ref, b2_ref,
                   o_ref, xbuf, sems):
    i = pl.program_id(0)
    last = pl.num_programs(0) - 1
    slot = lax.rem(i, 2)
    used = i < nu_ref[0]

    def issue(idx_ref, s):
        for r in range(MOE_BLOCK):
            _row_copy(x_hbm, xbuf.at[s], sems.at[s], idx_ref[0, 0, r], r).start()

    def drain(s):
        for r in range(MOE_BLOCK):
            _row_copy(x_hbm, xbuf.at[s], sems.at[s], 0, r).wait()

    @pl.when(i == 0)
    def _():
        issue(idx0_ref, 0)

    issue(idxn_ref, 1 - slot)
    drain(slot)

    @pl.when(i == last)
    def _():
        drain(1 - slot)

    @pl.when(jnp.logical_not(used))
    def _():
        o_ref[...] = jnp.zeros_like(o_ref)

    @pl.when(used)
    def _():
        xb = xbuf[slot].astype(BF16)
        h_glu = _dot(xb, w1g_ref[...]) + b1g_ref[...]
        h_lin = _dot(xb, w1l_ref[...]) + b1l_ref[...]
        x_glu = jnp.minimum(h_glu, SWIGLU_LIMIT)
        x_lin = jnp.clip(h_lin, -SWIGLU_LIMIT, SWIGLU_LIMIT)
        act = x_glu * _sigmoid(SWIGLU_ALPHA * x_glu) * (x_lin + 1.0)
        o_ref[...] = _dot(act.astype(BF16), w2_ref[...]) + b2_ref[...]


def _experts(x, row_tok, block_exp, n_used, w1g, w1l, b1g, b1l, w2, b2):
    d = x.shape[1]
    nblk = row_tok.shape[0] // MOE_BLOCK
    de = w1g.shape[2]
    idx = row_tok.reshape(nblk, 1, MOE_BLOCK)
    wmap = lambda i, be, nu: (be[i], 0, 0)
    grid_spec = pltpu.PrefetchScalarGridSpec(
        num_scalar_prefetch=2,
        grid=(nblk,),
        in_specs=[pl.BlockSpec((1, 1, MOE_BLOCK), lambda i, be, nu: (0, 0, 0), memory_space=pltpu.SMEM),
                  pl.BlockSpec((1, 1, MOE_BLOCK), lambda i, be, nu: (jnp.minimum(i + 1, nblk - 1), 0, 0),
                               memory_space=pltpu.SMEM),
                  pl.BlockSpec(memory_space=pl.ANY),
                  pl.BlockSpec((None, d, de), wmap), pl.BlockSpec((None, d, de), wmap),
                  pl.BlockSpec((None, 1, de), wmap), pl.BlockSpec((None, 1, de), wmap),
                  pl.BlockSpec((None, de, d), wmap), pl.BlockSpec((None, 1, d), wmap)],
        out_specs=pl.BlockSpec((MOE_BLOCK, d), lambda i, be, nu: (i, 0)),
        scratch_shapes=[pltpu.VMEM((2, MOE_BLOCK, d), x.dtype), pltpu.SemaphoreType.DMA((2,))],
    )
    return pl.pallas_call(
        _expert_kernel,
        grid_spec=grid_spec,
        out_shape=jax.ShapeDtypeStruct((nblk * MOE_BLOCK, d), F32),
        compiler_params=_cparams(1),
        name="moe_experts",
    )(block_exp, n_used, idx, idx, x, w1g, w1l, b1g, b1l, w2, b2)


def _combine_kernel(alpha, idx0_ref, idxn_ref, y_hbm, gate_ref, x_ref, lg_ref, lb_ref, o_ref, buf, sems):
    tc = x_ref.shape[0]
    i = pl.program_id(0)
    last = pl.num_programs(0) - 1
    slot = lax.rem(i, 2)

    def issue(idx_ref, s):
        for r in range(tc):
            for kk in range(TOP_K):
                _row_copy(y_hbm, buf.at[s, kk], sems.at[s], idx_ref[0, 0, r * TOP_K + kk], r).start()

    def drain(s):
        for r in range(tc * TOP_K):
            _row_copy(y_hbm, buf.at[s, 0], sems.at[s], 0, 0).wait()

    @pl.when(i == 0)
    def _():
        issue(idx0_ref, 0)

    issue(idxn_ref, 1 - slot)
    drain(slot)

    @pl.when(i == last)
    def _():
        drain(1 - slot)

    gates = gate_ref[...]
    y = buf[slot, 0] * gates[:, 0:1]
    for kk in range(1, TOP_K):
        y = y + buf[slot, kk] * gates[:, kk:kk + 1]
    o_ref[...] = _layer_norm(alpha * x_ref[...] + y, lg_ref[...], lb_ref[...])


def _combine(y_rows, dest, gates, x, ln_g, ln_b, alpha, tc):
    t, d = x.shape
    steps = t // tc
    row = lambda i: (i, 0)
    full = lambda a: pl.BlockSpec(a.shape, lambda i: (0,) * a.ndim)
    idx = dest.reshape(steps, 1, tc * TOP_K)
    return pl.pallas_call(
        functools.partial(_combine_kernel, alpha),
        grid=(steps,),
        in_specs=[pl.BlockSpec((1, 1, tc * TOP_K), lambda i: (0, 0, 0), memory_space=pltpu.SMEM),
                  pl.BlockSpec((1, 1, tc * TOP_K), lambda i: (jnp.minimum(i + 1, steps - 1), 0, 0),
                               memory_space=pltpu.SMEM),
                  pl.BlockSpec(memory_space=pl.ANY),
                  pl.BlockSpec((tc, LANES), row), pl.BlockSpec((tc, d), row), full(ln_g), full(ln_b)],
        out_specs=pl.BlockSpec((tc, d), row),
        out_shape=jax.ShapeDtypeStruct((t, d), F32),
        scratch_shapes=[pltpu.VMEM((2, TOP_K, tc, d), F32), pltpu.SemaphoreType.DMA((2,))],
        compiler_params=_cparams(1),
        name="moe_combine_ln2",
    )(idx, idx, y_rows, gates, x, ln_g, ln_b)


def _routing_tables(top_idx, rank, counts):
    t = top_idx.shape[0]
    n_assign = t * TOP_K
    e_flat = top_idx.reshape(n_assign)
    padded = ((counts + MOE_BLOCK - 1) // MOE_BLOCK) * MOE_BLOCK
    pend = jnp.cumsum(padded)
    pstart = pend - padded
    dest = (pstart[e_flat] + rank.reshape(n_assign)).astype(jnp.int32)
    n_blocks = -(-n_assign // MOE_BLOCK) + N_EXPERTS
    tok_flat = jnp.arange(n_assign, dtype=jnp.int32) // TOP_K
    row_tok = jnp.zeros((n_blocks * MOE_BLOCK,), jnp.int32).at[dest].set(tok_flat)
    block_start = jnp.arange(n_blocks, dtype=jnp.int32) * MOE_BLOCK
    block_exp = jnp.minimum(jnp.searchsorted(pend, block_start, side="right"), N_EXPERTS - 1).astype(jnp.int32)
    n_used = (pend[-1] // MOE_BLOCK).astype(jnp.int32).reshape(1)
    return dest, row_tok, block_exp, n_used


def _pad_rows(w, rows, at):
    out = jnp.zeros((rows, w.shape[1]), w.dtype)
    return out.at[at:at + w.shape[0]].set(w)


def _tile(n, pref):
    return pref if n % pref == 0 else n


def kernel(x, w_in, attn_sinks, w_attn_o, rwkv_mu, rwkv_w0, rwkv_w2, rwkv_a0, rwkv_a2, rwkv_g2, rwkv_k_k, rwkv_k_a, rwkv_r_k, rwkv_v0, rwkv_v1, rwkv_v2, rwkv_lnx_w, rwkv_lnx_b, w_rwkv_o, w_out, ln1_g, ln1_b, router_w, router_b, expert_w1, expert_b1, expert_w2, expert_b2, ln2_g, ln2_b):
    batch, seq, d = x.shape
    depth = w_in.shape[0]
    t = batch * seq
    alpha = (2.0 * depth) ** 0.25
    dr = RWKV_DIM
    tables = _rope_tables(seq)
    xf = x.reshape(t, d)
    v_first = None
    o_att, o_rw = ATT_Q + 2 * ATT_KV, ATT_Q + 2 * ATT_KV + 3 * dr + DECAY_LORA + AAA_LORA + GATE_LORA

    for l in range(depth):
        wl = w_in[l]
        wp = jnp.concatenate([
            wl[:, :ATT_Q], wl[:, o_att:o_att + 3 * dr], wl[:, o_rw:o_rw + 2 * d],
            wl[:, ATT_Q:o_att], wl[:, o_att + 3 * dr:o_rw]], axis=1).astype(BF16)
        p = _project(xf, wp, _tile(t, 512), P_COLS // 3)

        y_attn = _attention(p, attn_sinks[l], tables, w_attn_o[l].astype(BF16), batch, seq)

        mu = rwkv_mu[l]
        prm = {
            "mu_rkv": mu[:3 * dr].reshape(3, dr),
            "mu_lora": mu[3 * dr:].reshape(1, -1),
            "w0": rwkv_w0[l].reshape(1, dr), "a0": rwkv_a0[l].reshape(1, dr),
            "w2": _pad_rows(rwkv_w2[l], LANES, 0).astype(BF16),
            "a2": _pad_rows(rwkv_a2[l], LANES, DECAY_LORA).astype(BF16),
            "g2": rwkv_g2[l].astype(BF16),
        }
        if l > 0:
            prm["v0"] = rwkv_v0[l - 1].reshape(1, dr)
            prm["v1"] = jnp.zeros((dr, LANES), F32).at[:, :MV_LORA].set(rwkv_v1[l - 1]).astype(BF16)
            prm["v2"] = _pad_rows(rwkv_v2[l - 1], LANES, 0).astype(BF16)
        r_, k_, v_, lw_, a_, g_ = _rwkv_prep(p, prm, v_first, seq, _tile(seq, 256))
        if l == 0:
            v_first = v_
        scan_prm = jnp.stack([rwkv_k_k[l], rwkv_k_a[l], rwkv_r_k[l].reshape(dr), rwkv_lnx_w[l], rwkv_lnx_b[l],
                              jnp.zeros((dr,), F32), jnp.zeros((dr,), F32), jnp.zeros((dr,), F32)])
        z = _wkv_scan(r_, k_, v_, a_, g_, lw_, scan_prm, batch, seq)

        x1 = _merge(z, y_attn, p, xf, w_rwkv_o[l].astype(BF16), w_out[l].astype(BF16),
                    ln1_g[l].reshape(1, d), ln1_b[l].reshape(1, d), alpha, _tile(t, 256))

        rw = jnp.zeros((d, LANES), F32).at[:, :N_EXPERTS].set(router_w[l])
        rb = jnp.full((1, LANES), -1e30, F32).at[0, :N_EXPERTS].set(router_b[l])
        top_idx, gates, rank, counts = _router(x1, jnp.stack(_split3(rw)), rb, _tile(t, 512))
        dest, row_tok, block_exp, n_used = _routing_tables(top_idx[:, :TOP_K], rank[:, :TOP_K],
                                                           counts[0, :N_EXPERTS])

        b1 = expert_b1[l]
        w1g, w1l = _deinterleave(expert_w1[l].reshape(N_EXPERTS * d, 2 * D_EXPERT), 512)
        y_rows = _experts(x1, row_tok, block_exp, n_used,
                          w1g.reshape(N_EXPERTS, d, D_EXPERT), w1l.reshape(N_EXPERTS, d, D_EXPERT),
                          b1[:, None, 0::2], b1[:, None, 1::2],
                          expert_w2[l].astype(BF16), expert_b2[l][:, None, :])
        xf = _combine(y_rows, dest, gates, x1, ln2_g[l].reshape(1, d), ln2_b[l].reshape(1, d),
                      alpha, _tile(t, 128))
    return xf.reshape(batch, seq, d)
```

```python
import functools

import jax
import jax.numpy as jnp
from jax import lax
from jax.experimental import pallas as pl
from jax.experimental.pallas import tpu as pltpu

F32 = jnp.float32
BF16 = jnp.bfloat16

D_MODEL = 1024
HEAD_DIM = 64
N_Q_HEADS = 16
N_KV_HEADS = 4
WINDOW = 128
ATTN_BLOCK = 128
ROPE_THETA = 10000.0
ATT_Q = N_Q_HEADS * HEAD_DIM
ATT_KV = N_KV_HEADS * HEAD_DIM

RWKV_HEAD = 64
RWKV_HEADS = 16
RWKV_DIM = RWKV_HEADS * RWKV_HEAD
DECAY_LORA = 64
AAA_LORA = 64
MV_LORA = 32
GATE_LORA = 128
RWKV_GN_EPS = 64e-5

N_EXPERTS = 32
TOP_K = 4
D_EXPERT = 1024
SWIGLU_LIMIT = 7.0
SWIGLU_ALPHA = 1.702
MOE_BLOCK = 256

LN_EPS = 1e-5

LANES = 128
HEAD_PAIRS = RWKV_HEADS // 2
SCAN_CHUNK = 64
SCAN_CHUNKS_PER_STEP = 2
VMEM_LIMIT = 56 * 1024 * 1024

COL_Q, COL_R, COL_K, COL_V, COL_GA, COL_GB = 0, 1024, 2048, 3072, 4096, 5120
COL_AK, COL_AV, COL_LORA = 6144, 6400, 6656
P_COLS = 6912


def _cparams(n_axes):
    return pltpu.CompilerParams(dimension_semantics=("arbitrary",) * n_axes,
                                vmem_limit_bytes=VMEM_LIMIT)


def _dot(a, b):
    return jnp.dot(a, b, preferred_element_type=F32)


def _dot_nt(a, b):
    return lax.dot_general(a, b, (((1,), (1,)), ((), ())), preferred_element_type=F32)


def _dot_tn(a, b):
    return lax.dot_general(a, b, (((0,), (0,)), ((), ())), preferred_element_type=F32)


def _sigmoid(z):
    return 1.0 / (1.0 + jnp.exp(-z))


def _layer_norm(h, g, b):
    mu = jnp.mean(h, axis=-1, keepdims=True)
    c = h - mu
    var = jnp.mean(c * c, axis=-1, keepdims=True)
    return c * lax.rsqrt(var + LN_EPS) * g + b


def _proj_kernel(x_ref, w_ref, o_ref):
    o_ref[...] = _dot(x_ref[...].astype(BF16), w_ref[...]).astype(o_ref.dtype)


def _project(x, w, tm, tn):
    m, k = x.shape
    n = w.shape[1]
    return pl.pallas_call(
        _proj_kernel,
        grid=(n // tn, m // tm),
        in_specs=[pl.BlockSpec((tm, k), lambda j, i: (i, 0)),
                  pl.BlockSpec((k, tn), lambda j, i: (0, j))],
        out_specs=pl.BlockSpec((tm, tn), lambda j, i: (i, j)),
        out_shape=jax.ShapeDtypeStruct((m, n), BF16),
        compiler_params=_cparams(2),
        name="proj",
    )(x, w)


def _rope(x, c, s1, s2):
    return x * c + pltpu.roll(x, LANES - HEAD_DIM // 2, 1) * s1 + pltpu.roll(x, HEAD_DIM // 2, 1) * s2


def _attn_kernel(sink_ref, q_ref, kc_ref, kp_ref, vc_ref, vp_ref, tc_ref, tp_ref, wo_ref, o_ref):
    n = pl.program_id(1)
    blk = ATTN_BLOCK
    lane = lax.broadcasted_iota(jnp.int32, (1, LANES), 1)
    lo = lane < HEAD_DIM
    cq, s1q, s2q = tc_ref[0], tc_ref[1], tc_ref[2]
    cp, s1p, s2p = tp_ref[0], tp_ref[1], tp_ref[2]

    row = lax.broadcasted_iota(jnp.int32, (blk, 2 * blk), 0)
    col = lax.broadcasted_iota(jnp.int32, (blk, 2 * blk), 1)
    valid = (col > row) & (col <= row + WINDOW) & ((n > 0) | (col >= blk))

    def lo_hi(t, off):
        sw = pltpu.roll(t, HEAD_DIM, 1)
        t_lo = jnp.where(lo, t if off == 0 else sw, 0.0).astype(BF16)
        t_hi = jnp.where(lo, 0.0, sw if off == 0 else t).astype(BF16)
        return t_lo, t_hi

    kv_tiles = []
    for c in range(ATT_KV // LANES):
        sl = slice(c * LANES, (c + 1) * LANES)
        kc = _rope(kc_ref[:, sl].astype(F32), cq, s1q, s2q)
        kp = _rope(kp_ref[:, sl].astype(F32), cp, s1p, s2p)
        kt = jnp.concatenate([kp, kc], axis=0)
        vt = jnp.concatenate([vp_ref[:, sl].astype(F32), vc_ref[:, sl].astype(F32)], axis=0)
        kv_tiles.append((kt, vt))

    k_m, v_st = [], []
    for g in range(N_KV_HEADS):
        kt, vt = kv_tiles[g // 2]
        k_m.append(lo_hi(kt, g % 2))
        v_st.append(jnp.concatenate(lo_hi(vt, g % 2), axis=0))
    n_pairs = N_Q_HEADS // 2
    qp = [(_rope(q_ref[:, j * LANES:(j + 1) * LANES].astype(F32), cq, s1q, s2q) * (HEAD_DIM ** -0.5)).astype(BF16)
          for j in range(n_pairs)]
    scores = [_dot_nt(qp[hh // 2], k_m[hh // 4][hh % 2]) for hh in range(N_Q_HEADS)]
    probs = []
    for hh in range(N_Q_HEADS):
        s = jnp.where(valid, scores[hh], -1e30)
        sink = sink_ref[hh]
        m = jnp.maximum(jnp.max(s, axis=-1, keepdims=True), sink)
        p = jnp.exp(s - m)
        den = jnp.sum(p, axis=-1, keepdims=True) + jnp.exp(sink - m)
        probs.append((p / den).astype(BF16))
    outs = [_dot(jnp.concatenate([probs[2 * j], probs[2 * j + 1]], axis=1), v_st[j // 2]) for j in range(n_pairs)]
    att = jnp.concatenate(outs, axis=1).astype(BF16)
    o_ref[...] = _dot(att, wo_ref[...]).astype(o_ref.dtype)


def _attention(p, sinks, tables, w_o, batch, seq):
    nb = seq // ATTN_BLOCK
    blk = ATTN_BLOCK
    kcol, vcol = COL_AK // ATT_KV, COL_AV // ATT_KV
    cur = lambda b, n: b * nb + n
    prev = lambda b, n: b * nb + jnp.maximum(n - 1, 0)
    return pl.pallas_call(
        _attn_kernel,
        grid=(batch, nb),
        in_specs=[
            pl.BlockSpec(memory_space=pltpu.SMEM),
            pl.BlockSpec((blk, ATT_Q), lambda b, n: (cur(b, n), COL_Q // ATT_Q)),
            pl.BlockSpec((blk, ATT_KV), lambda b, n: (cur(b, n), kcol)),
            pl.BlockSpec((blk, ATT_KV), lambda b, n: (prev(b, n), kcol)),
            pl.BlockSpec((blk, ATT_KV), lambda b, n: (cur(b, n), vcol)),
            pl.BlockSpec((blk, ATT_KV), lambda b, n: (prev(b, n), vcol)),
            pl.BlockSpec((3, blk, LANES), lambda b, n: (0, n, 0)),
            pl.BlockSpec((3, blk, LANES), lambda b, n: (0, jnp.maximum(n - 1, 0), 0)),
            pl.BlockSpec((ATT_Q, D_MODEL), lambda b, n: (0, 0)),
        ],
        out_specs=pl.BlockSpec((blk, D_MODEL), lambda b, n: (cur(b, n), 0)),
        out_shape=jax.ShapeDtypeStruct((batch * seq, D_MODEL), BF16),
        compiler_params=_cparams(2),
        name="attn",
    )(sinks, p, p, p, p, p, tables, tables, w_o)


def _rope_tables(seq):
    pos = jnp.arange(seq, dtype=F32)
    inv = ROPE_THETA ** (-jnp.arange(0, HEAD_DIM, 2, dtype=F32) / HEAD_DIM)
    ang = pos[:, None] * inv[None, :]
    cos, sin = jnp.cos(ang), jnp.sin(ang)
    zero = jnp.zeros_like(sin)
    c = jnp.tile(cos, (1, LANES // (HEAD_DIM // 2)))
    s1 = jnp.tile(jnp.concatenate([-sin, zero], axis=1), (1, LANES // HEAD_DIM))
    s2 = jnp.tile(jnp.concatenate([zero, sin], axis=1), (1, LANES // HEAD_DIM))
    return jnp.stack([c, s1, s2])


def _prep_kernel(has_vres, seq, *refs):
    if has_vres:
        (r_ref, k_ref, v_ref, l_ref, rp_ref, kp_ref, vp_ref, lp_ref, mu_ref, mul_ref, w0_ref, a0_ref,
         w2_ref, a2_ref, g2_ref, vf_ref, v0_ref, v1_ref, v2_ref,
         ro_ref, ko_ref, vo_ref, lw_ref, ao_ref, go_ref) = refs
    else:
        (r_ref, k_ref, v_ref, l_ref, rp_ref, kp_ref, vp_ref, lp_ref, mu_ref, mul_ref, w0_ref, a0_ref,
         w2_ref, a2_ref, g2_ref,
         ro_ref, ko_ref, vo_ref, lw_ref, ao_ref, go_ref) = refs
    tm = r_ref.shape[0]
    i = pl.program_id(0)
    not_first = (lax.rem(i * tm, seq) != 0).astype(F32)
    row0 = lax.broadcasted_iota(jnp.int32, (tm, 1), 0) == 0

    def shift_mix(cur_ref, prev_ref, mu):
        cur = cur_ref[...].astype(F32)
        last = prev_ref[7:8, :].astype(F32) * not_first
        sh = jnp.where(row0, last, pltpu.roll(cur, 1, 0))
        return cur + (sh - cur) * mu

    r = shift_mix(r_ref, rp_ref, mu_ref[0:1, :])
    k = shift_mix(k_ref, kp_ref, mu_ref[1:2, :])
    v = shift_mix(v_ref, vp_ref, mu_ref[2:3, :])
    xl = shift_mix(l_ref, lp_ref, mul_ref[...])
    wa = xl[:, :LANES]
    zw = w0_ref[...] + _dot(jnp.tanh(wa).astype(BF16), w2_ref[...])
    lw_ref[...] = -jnp.exp(-0.5) * _sigmoid(zw)
    a = _sigmoid(a0_ref[...] + _dot(wa.astype(BF16), a2_ref[...]))
    g = _dot(_sigmoid(xl[:, LANES:]).astype(BF16), g2_ref[...])
    if has_vres:
        lo_rank = _dot(v.astype(BF16), v1_ref[...])
        mix = _sigmoid(v0_ref[...] + _dot(lo_rank.astype(BF16), v2_ref[...]))
        v = v + (vf_ref[...].astype(F32) - v) * mix
    ro_ref[...] = r.astype(ro_ref.dtype)
    ko_ref[...] = k.astype(ko_ref.dtype)
    vo_ref[...] = v.astype(vo_ref.dtype)
    ao_ref[...] = a.astype(ao_ref.dtype)
    go_ref[...] = g.astype(go_ref.dtype)


def _rwkv_prep(p, prm, v_first, seq, tm):
    t = p.shape[0]
    dr = RWKV_DIM
    has_vres = v_first is not None
    lw_cols = 2 * LANES
    row = lambda i: (i, 0)
    prev8 = lambda cb: (lambda i: (jnp.maximum(i * (tm // 8) - 1, 0), cb))
    full = lambda a: pl.BlockSpec(a.shape, lambda i: (0,) * a.ndim)
    in_specs = [
        pl.BlockSpec((tm, dr), lambda i: (i, COL_R // dr)),
        pl.BlockSpec((tm, dr), lambda i: (i, COL_K // dr)),
        pl.BlockSpec((tm, dr), lambda i: (i, COL_V // dr)),
        pl.BlockSpec((tm, lw_cols), lambda i: (i, COL_LORA // lw_cols)),
        pl.BlockSpec((8, dr), prev8(COL_R // dr)),
        pl.BlockSpec((8, dr), prev8(COL_K // dr)),
        pl.BlockSpec((8, dr), prev8(COL_V // dr)),
        pl.BlockSpec((8, lw_cols), prev8(COL_LORA // lw_cols)),
    ]
    args = [p] * 8
    small = [prm["mu_rkv"], prm["mu_lora"], prm["w0"], prm["a0"], prm["w2"], prm["a2"], prm["g2"]]
    in_specs += [full(a) for a in small]
    args += small
    if has_vres:
        small_v = [prm["v0"], prm["v1"], prm["v2"]]
        in_specs += [pl.BlockSpec((tm, dr), row)] + [full(a) for a in small_v]
        args += [v_first] + small_v
    out_shape = [jax.ShapeDtypeStruct((t, dr), BF16)] * 3 + [jax.ShapeDtypeStruct((t, dr), F32)] + \
                [jax.ShapeDtypeStruct((t, dr), BF16)] * 2
    return pl.pallas_call(
        functools.partial(_prep_kernel, has_vres, seq),
        grid=(t // tm,),
        in_specs=in_specs,
        out_specs=[pl.BlockSpec((tm, dr), row)] * 6,
        out_shape=out_shape,
        compiler_params=_cparams(1),
        name="rwkv_prep",
    )(*args)


def _scan_kernel(r_ref, k_ref, v_ref, a_ref, g_ref, lw_ref, prm_ref, o_ref, state_ref):
    ch = SCAN_CHUNK
    c_id = pl.program_id(1)

    @pl.when(c_id == 0)
    def _():
        state_ref[...] = jnp.zeros_like(state_ref)

    n_chunks = lw_ref.shape[0] // ch
    chunk_rows = [slice(cc * ch, (cc + 1) * ch) for cc in range(n_chunks)]
    tri = (lax.broadcasted_iota(jnp.int32, (ch, ch), 0) >= lax.broadcasted_iota(jnp.int32, (ch, ch), 1)
           ).astype(BF16)
    e_in, e_ex, e_inv, e_end, g_tot, e_ref = [], [], [], [], [], []
    for rows in chunk_rows:
        lw = lw_ref[rows, :]
        hi = lw.astype(BF16)
        rem = lw - hi.astype(F32)
        mid = rem.astype(BF16)
        low = (rem - mid.astype(F32)).astype(BF16)
        cum = _dot(tri, hi) + _dot(tri, mid) + _dot(tri, low)
        ref_row = cum[ch // 2 - 1:ch // 2, :]
        total = cum[ch - 1:ch, :]
        e_in.append(jnp.exp(cum - ref_row))
        e_ex.append(jnp.exp(cum - lw - ref_row))
        e_inv.append(jnp.exp(ref_row - cum))
        e_end.append(jnp.exp(total - cum))
        g_tot.append(jnp.exp(total))
        e_ref.append(jnp.exp(ref_row))

    lane = lax.broadcasted_iota(jnp.int32, (1, LANES), 1)
    lo = lane < RWKV_HEAD
    r2 = lax.broadcasted_iota(jnp.int32, (2 * ch, 2 * ch), 0)
    c2 = lax.broadcasted_iota(jnp.int32, (2 * ch, 2 * ch), 1)
    same = ((r2 < ch) & (c2 < ch)) | ((r2 >= ch) & (c2 >= ch))
    tri_strict = same & (r2 > c2)
    tri_incl = same & (r2 >= c2)
    eye = jnp.where(r2 == c2, 1.0, 0.0)
    rb = lax.broadcasted_iota(jnp.int32, (2 * ch, LANES), 0)
    cb = lax.broadcasted_iota(jnp.int32, (2 * ch, LANES), 1)
    blockmask = ((rb < ch) & (cb < RWKV_HEAD)) | ((rb >= ch) & (cb >= RWKV_HEAD))

    def stack(x):
        return jnp.concatenate([jnp.where(lo, x, 0.0), jnp.where(lo, 0.0, x)], axis=0)

    def head_sums(x):
        s_e = jnp.sum(jnp.where(lo, x, 0.0), axis=-1, keepdims=True)
        s_o = jnp.sum(jnp.where(lo, 0.0, x), axis=-1, keepdims=True)
        return jnp.where(lo, s_e, s_o)

    sls = [slice(pr * LANES, (pr + 1) * LANES) for pr in range(HEAD_PAIRS)]
    items = [(cc, pr) for cc in range(n_chunks) for pr in range(HEAD_PAIRS)]
    pairs = range(len(items))

    def kmod_of(rows, sl):
        return k_ref[rows, sl].astype(F32) * (1.0 + (a_ref[rows, sl].astype(F32) - 1.0) * prm_ref[1:2, sl])

    abar, rbar, vs, lhs, rhs, upd = [], [], [], [], [], []
    for cc, pr in items:
        rows, sl = chunk_rows[cc], sls[pr]
        a = a_ref[rows, sl].astype(F32)
        kx = k_ref[rows, sl].astype(F32) * prm_ref[0:1, sl]
        kk = kx / jnp.maximum(jnp.sqrt(head_sums(kx * kx)), 1e-12)
        beta = kk * a
        kmod = kmod_of(rows, sl)
        ab = stack(-kk * e_ex[cc][:, sl]).astype(BF16)
        rb_ = stack(r_ref[rows, sl].astype(F32) * e_in[cc][:, sl]).astype(BF16)
        abar.append(ab)
        rbar.append(rb_)
        lhs.append(jnp.concatenate([ab, rb_], axis=0))
        rhs.append(jnp.concatenate([stack(beta * e_inv[cc][:, sl]).astype(BF16),
                                    stack(kmod * e_inv[cc][:, sl]).astype(BF16)], axis=0))
        upd.append(jnp.concatenate([stack(beta * e_end[cc][:, sl]).astype(BF16),
                                    stack(kmod * e_end[cc][:, sl]).astype(BF16)], axis=0))
        vs.append(stack(v_ref[rows, sl].astype(F32)).astype(BF16))

    grams = [_dot_nt(lhs[pr], rhs[pr]) for pr in pairs]
    a_ak = [jnp.where(tri_strict, gm[:2 * ch, 2 * ch:], 0.0).astype(BF16) for gm in grams]
    m_r = [jnp.concatenate([jnp.where(tri_incl, gm[2 * ch:, :2 * ch], 0.0),
                            jnp.where(tri_incl, gm[2 * ch:, 2 * ch:], 0.0)], axis=1).astype(BF16) for gm in grams]

    a_ab = [jnp.where(tri_strict, gm[:2 * ch, :2 * ch], 0.0) for gm in grams]
    pw = [x.astype(BF16) for x in a_ab]
    pw = [_dot(x, x).astype(BF16) for x in pw]
    tinv = [eye + x for x in a_ab]
    for _ in range((ch - 1).bit_length() - 2):
        both = [_dot(pw[pr], jnp.concatenate([pw[pr], tinv[pr].astype(BF16)], axis=1)) for pr in pairs]
        tinv = [tinv[pr] + both[pr][:, 2 * ch:] for pr in pairs]
        pw = [both[pr][:, :2 * ch].astype(BF16) for pr in pairs]
    tinv = [tinv[pr] + _dot(pw[pr], tinv[pr].astype(BF16)) for pr in pairs]

    heads = range(HEAD_PAIRS)
    st = [state_ref[pr] for pr in heads]
    o_all = []
    for cc in range(n_chunks):
        it = [cc * HEAD_PAIRS + pr for pr in heads]
        st_b = [(st[pr] * e_ref[cc][:, sls[pr]]).astype(BF16) for pr in heads]
        w = [_dot_nt(abar[it[pr]], st_b[pr]) + _dot(a_ak[it[pr]], vs[it[pr]]) for pr in heads]
        u = [_dot(tinv[it[pr]].astype(BF16), w[pr].astype(BF16)) for pr in heads]
        uv = [jnp.concatenate([u[pr].astype(BF16), vs[it[pr]]], axis=0) for pr in heads]
        o_all.append([_dot_nt(rbar[it[pr]], st_b[pr]) + _dot(m_r[it[pr]], uv[pr]) for pr in heads])
        st = [st[pr] * g_tot[cc][:, sls[pr]] + _dot_tn(uv[pr], upd[it[pr]]) for pr in heads]
    for pr in heads:
        state_ref[pr] = st[pr]

    for cc, rows in enumerate(chunk_rows):
        for pr, sl in enumerate(sls):
            o_s = o_all[cc][pr]
            mean = jnp.sum(o_s, axis=-1, keepdims=True) * (1.0 / RWKV_HEAD)
            cen = jnp.where(blockmask, o_s - mean, 0.0)
            var = jnp.sum(cen * cen, axis=-1, keepdims=True) * (1.0 / RWKV_HEAD)
            gn_s = cen * lax.rsqrt(var + RWKV_GN_EPS)
            gn = gn_s[:ch] + gn_s[ch:]
            v = v_ref[rows, sl].astype(F32)
            bonus = head_sums(r_ref[rows, sl].astype(F32) * kmod_of(rows, sl) * prm_ref[2:3, sl]) * v
            z = (gn * prm_ref[3:4, sl] + prm_ref[4:5, sl] + bonus) * g_ref[rows, sl].astype(F32)
            o_ref[rows, sl] = z.astype(o_ref.dtype)


def _wkv_scan(r, k, v, a, g, lw, prm, batch, seq):
    rows = SCAN_CHUNK * SCAN_CHUNKS_PER_STEP
    nc = seq // rows
    dr = RWKV_DIM
    blk = pl.BlockSpec((rows, dr), lambda b, c: (b * nc + c, 0))
    return pl.pallas_call(
        _scan_kernel,
        grid=(batch, nc),
        in_specs=[blk] * 6 + [pl.BlockSpec(prm.shape, lambda b, c: (0, 0))],
        out_specs=blk,
        out_shape=jax.ShapeDtypeStruct((batch * seq, dr), BF16),
        scratch_shapes=[pltpu.VMEM((HEAD_PAIRS, 2 * RWKV_HEAD, LANES), F32)],
        compiler_params=_cparams(2),
        name="wkv_scan",
    )(r, k, v, a, g, lw, prm)


def _merge_kernel(alpha, z_ref, ya_ref, ga_ref, gb_ref, x_ref, wr_ref, wo_ref, lg_ref, lb_ref, o_ref, ot_ref):
    y_r = _dot(z_ref[...], wr_ref[...])
    mix = _sigmoid(ga_ref[...].astype(F32)) * ya_ref[...].astype(F32) + _sigmoid(gb_ref[...].astype(F32)) * y_r
    mixed = _dot(mix.astype(BF16), wo_ref[...])
    out = _layer_norm(alpha * x_ref[...] + mixed, lg_ref[...], lb_ref[...])
    o_ref[...] = out
    _store_token_rows(ot_ref, out)


def _merge(z, y_attn, p, x, w_r, w_o, ln_g, ln_b, alpha, tm):
    t, d = x.shape
    row = lambda i: (i, 0)
    full = lambda a: pl.BlockSpec(a.shape, lambda i: (0,) * a.ndim)
    return pl.pallas_call(
        functools.partial(_merge_kernel, alpha),
        grid=(t // tm,),
        in_specs=[pl.BlockSpec((tm, d), row), pl.BlockSpec((tm, d), row),
                  pl.BlockSpec((tm, d), lambda i: (i, COL_GA // d)),
                  pl.BlockSpec((tm, d), lambda i: (i, COL_GB // d)),
                  pl.BlockSpec((tm, d), row), full(w_r), full(w_o), full(ln_g), full(ln_b)],
        out_specs=[pl.BlockSpec((tm, d), row), pl.BlockSpec((tm * SUBLANES, LANES), row)],
        out_shape=[jax.ShapeDtypeStruct((t, d), F32), jax.ShapeDtypeStruct((t * SUBLANES, LANES), F32)],
        compiler_params=_cparams(1),
        name="merge_ln1",
    )(z, y_attn, p, p, x, w_r, w_o, ln_g, ln_b)


def _split3(x):
    hi = x.astype(BF16)
    rem = x - hi.astype(F32)
    mid = rem.astype(BF16)
    low = (rem - mid.astype(F32)).astype(BF16)
    return hi, mid, low


def _router_kernel(x_ref, w_ref, b_ref, idx_ref, gate_ref, rank_ref, cnt_ref, carry_ref):
    @pl.when(pl.program_id(0) == 0)
    def _():
        carry_ref[...] = jnp.zeros_like(carry_ref)

    x0, x1, x2 = _split3(x_ref[...])
    w0, w1, w2 = w_ref[0], w_ref[1], w_ref[2]
    logits = (_dot(x0, w0) + (_dot(x0, w1) + _dot(x1, w0)) +
              (_dot(x0, w2) + _dot(x1, w1) + _dot(x2, w0))) + b_ref[...]
    tm = logits.shape[0]
    lane = lax.broadcasted_iota(jnp.int32, (tm, LANES), 1).astype(F32)
    work = logits
    idx_out = jnp.zeros((tm, LANES), F32)
    val_out = jnp.zeros((tm, LANES), F32)
    top = None
    den = jnp.zeros((tm, 1), F32)
    hits = []
    for kk in range(TOP_K):
        m = jnp.max(work, axis=-1, keepdims=True)
        sel = jnp.min(jnp.where(work == m, lane, float(LANES)), axis=-1, keepdims=True)
        if kk == 0:
            top = m
        e = jnp.exp(m - top)
        den = den + e
        idx_out = jnp.where(lane == float(kk), sel, idx_out)
        val_out = jnp.where(lane == float(kk), e, val_out)
        hits.append(lane == sel)
        work = jnp.where(hits[-1], -jnp.inf, work)
    idx_ref[...] = idx_out.astype(jnp.int32)
    gate_ref[...] = val_out / den

    chosen = jnp.where(hits[0] | hits[1] | hits[2] | hits[3], 1.0, 0.0)
    r_i = lax.broadcasted_iota(jnp.int32, (tm, tm), 0)
    c_i = lax.broadcasted_iota(jnp.int32, (tm, tm), 1)
    earlier = jnp.where(r_i > c_i, 1.0, 0.0).astype(BF16)
    before = carry_ref[0:1, :] + _dot(earlier, chosen.astype(BF16))
    rank_out = jnp.zeros((tm, LANES), F32)
    for kk in range(TOP_K):
        rk = jnp.sum(jnp.where(hits[kk], before, 0.0), axis=-1, keepdims=True)
        rank_out = jnp.where(lane == float(kk), rk, rank_out)
    rank_ref[...] = rank_out.astype(jnp.int32)
    carry_ref[0:1, :] = carry_ref[0:1, :] + jnp.sum(chosen, axis=0, keepdims=True)
    cnt_ref[...] = carry_ref[...].astype(jnp.int32)


def _router(x, w3, b, tm):
    t, d = x.shape
    row = lambda i: (i, 0)
    return pl.pallas_call(
        _router_kernel,
        grid=(t // tm,),
        in_specs=[pl.BlockSpec((tm, d), row), pl.BlockSpec(w3.shape, lambda i: (0, 0, 0)),
                  pl.BlockSpec(b.shape, lambda i: (0, 0))],
        out_specs=[pl.BlockSpec((tm, LANES), row)] * 3 + [pl.BlockSpec((8, LANES), lambda i: (0, 0))],
        out_shape=[jax.ShapeDtypeStruct((t, LANES), jnp.int32), jax.ShapeDtypeStruct((t, LANES), F32),
                   jax.ShapeDtypeStruct((t, LANES), jnp.int32), jax.ShapeDtypeStruct((8, LANES), jnp.int32)],
        scratch_shapes=[pltpu.VMEM((8, LANES), F32)],
        compiler_params=_cparams(1),
        name="router",
    )(x, w3, b)


SUBLANES = 8
TOKEN_TILES = D_MODEL // LANES


def _row_copy(src_hbm, dst_vmem, sem, src_row8, dst_tok):
    if not isinstance(src_row8, int):
        src_row8 = pl.multiple_of(src_row8, SUBLANES)
    return pltpu.make_async_copy(src_hbm.at[pl.ds(src_row8, SUBLANES), :],
                                 dst_vmem.at[pl.ds(dst_tok * SUBLANES, SUBLANES), :], sem)


def _load_token_rows(ref, n):
    return jnp.concatenate([ref[pl.ds(s, n, stride=SUBLANES), :] for s in range(TOKEN_TILES)], axis=1)


def _store_token_rows(ref, val):
    n = val.shape[0]
    for s in range(TOKEN_TILES):
        ref[pl.ds(s, n, stride=SUBLANES), :] = val[:, s * LANES:(s + 1) * LANES]


DEINT_GROUP = 2 * LANES


def _deint_kernel(w_ref, sel_ref, g_ref, l_ref):
    for c in range(w_ref.shape[1] // DEINT_GROUP):
        blk = w_ref[:, c * DEINT_GROUP:(c + 1) * DEINT_GROUP].astype(BF16)
        both = _dot(blk, sel_ref[...])
        g_ref[:, c * LANES:(c + 1) * LANES] = both[:, :LANES].astype(BF16)
        l_ref[:, c * LANES:(c + 1) * LANES] = both[:, LANES:].astype(BF16)


def _deinterleave(w, tm):
    rows, cols = w.shape
    src = jnp.arange(DEINT_GROUP)[:, None]
    dst = jnp.arange(DEINT_GROUP)[None, :]
    sel = jnp.where(dst < LANES, src == 2 * dst, src == 2 * (dst - LANES) + 1).astype(BF16)
    half = jax.ShapeDtypeStruct((rows, cols // 2), BF16)
    return pl.pallas_call(
        _deint_kernel,
        grid=(rows // tm,),
        in_specs=[pl.BlockSpec((tm, cols), lambda i: (i, 0)), pl.BlockSpec(sel.shape, lambda i: (0, 0))],
        out_specs=[pl.BlockSpec((tm, cols // 2), lambda i: (i, 0))] * 2,
        out_shape=[half, half],
        compiler_params=_cparams(1),
        name="deinterleave",
    )(w, sel)


def _expert_kernel(be_ref, nu_ref, idx0_ref, idxn_ref, x_hbm, w1g_ref, w1l_ref, b1g_ref, b1l_ref, w2_ref, b2_ref,
                   o_ref, xbuf, sems):
    i = pl.program_id(0)
    last = pl.num_programs(0) - 1
    slot = lax.rem(i, 2)
    used = i < nu_ref[0]

    def issue(idx_ref, s):
        for r in range(MOE_BLOCK):
            _row_copy(x_hbm, xbuf.at[s], sems.at[s], idx_ref[0, 0, r], r).start()

    def drain(s):
        for r in range(MOE_BLOCK):
            _row_copy(x_hbm, xbuf.at[s], sems.at[s], 0, r).wait()

    @pl.when(i == 0)
    def _():
        issue(idx0_ref, 0)

    issue(idxn_ref, 1 - slot)
    drain(slot)

    @pl.when(i == last)
    def _():
        drain(1 - slot)

    @pl.when(jnp.logical_not(used))
    def _():
        o_ref[...] = jnp.zeros_like(o_ref)

    @pl.when(used)
    def _():
        xb = _load_token_rows(xbuf.at[slot], MOE_BLOCK).astype(BF16)
        h_glu = _dot(xb, w1g_ref[...]) + b1g_ref[...]
        h_lin = _dot(xb, w1l_ref[...]) + b1l_ref[...]
        x_glu = jnp.minimum(h_glu, SWIGLU_LIMIT)
        x_lin = jnp.clip(h_lin, -SWIGLU_LIMIT, SWIGLU_LIMIT)
        act = x_glu * _sigmoid(SWIGLU_ALPHA * x_glu) * (x_lin + 1.0)
        _store_token_rows(o_ref, _dot(act.astype(BF16), w2_ref[...]) + b2_ref[...])


def _experts(x_tiles, row_tok, block_exp, n_used, w1g, w1l, b1g, b1l, w2, b2):
    d = D_MODEL
    nblk = row_tok.shape[0] // MOE_BLOCK
    de = w1g.shape[2]
    idx = (row_tok * SUBLANES).reshape(nblk, 1, MOE_BLOCK)
    blk_rows = MOE_BLOCK * SUBLANES
    wmap = lambda i, be, nu: (be[i], 0, 0)
    grid_spec = pltpu.PrefetchScalarGridSpec(
        num_scalar_prefetch=2,
        grid=(nblk,),
        in_specs=[pl.BlockSpec((1, 1, MOE_BLOCK), lambda i, be, nu: (0, 0, 0), memory_space=pltpu.SMEM),
                  pl.BlockSpec((1, 1, MOE_BLOCK), lambda i, be, nu: (jnp.minimum(i + 1, nblk - 1), 0, 0),
                               memory_space=pltpu.SMEM),
                  pl.BlockSpec(memory_space=pl.ANY),
                  pl.BlockSpec((None, d, de), wmap), pl.BlockSpec((None, d, de), wmap),
                  pl.BlockSpec((None, 1, de), wmap), pl.BlockSpec((None, 1, de), wmap),
                  pl.BlockSpec((None, de, d), wmap), pl.BlockSpec((None, 1, d), wmap)],
        out_specs=pl.BlockSpec((blk_rows, LANES), lambda i, be, nu: (i, 0)),
        scratch_shapes=[pltpu.VMEM((2, blk_rows, LANES), F32), pltpu.SemaphoreType.DMA((2,))],
    )
    return pl.pallas_call(
        _expert_kernel,
        grid_spec=grid_spec,
        out_shape=jax.ShapeDtypeStruct((nblk * blk_rows, LANES), F32),
        compiler_params=_cparams(1),
        name="moe_experts",
    )(block_exp, n_used, idx, idx, x_tiles, w1g, w1l, b1g, b1l, w2, b2)


def _combine_kernel(alpha, idx0_ref, idxn_ref, y_hbm, gate_ref, x_ref, lg_ref, lb_ref, o_ref, buf, sems):
    tc = x_ref.shape[0]
    i = pl.program_id(0)
    last = pl.num_programs(0) - 1
    slot = lax.rem(i, 2)

    def issue(idx_ref, s):
        for r in range(tc):
            for kk in range(TOP_K):
                _row_copy(y_hbm, buf.at[s, kk], sems.at[s], idx_ref[0, 0, r * TOP_K + kk], r).start()

    def drain(s):
        for r in range(tc * TOP_K):
            _row_copy(y_hbm, buf.at[s, 0], sems.at[s], 0, 0).wait()

    @pl.when(i == 0)
    def _():
        issue(idx0_ref, 0)

    issue(idxn_ref, 1 - slot)
    drain(slot)

    @pl.when(i == last)
    def _():
        drain(1 - slot)

    gates = gate_ref[...]
    y = _load_token_rows(buf.at[slot, 0], tc) * gates[:, 0:1]
    for kk in range(1, TOP_K):
        y = y + _load_token_rows(buf.at[slot, kk], tc) * gates[:, kk:kk + 1]
    o_ref[...] = _layer_norm(alpha * x_ref[...] + y, lg_ref[...], lb_ref[...])


def _combine(y_rows, dest, gates, x, ln_g, ln_b, alpha, tc):
    t, d = x.shape
    steps = t // tc
    row = lambda i: (i, 0)
    full = lambda a: pl.BlockSpec(a.shape, lambda i: (0,) * a.ndim)
    idx = (dest * SUBLANES).reshape(steps, 1, tc * TOP_K)
    return pl.pallas_call(
        functools.partial(_combine_kernel, alpha),
        grid=(steps,),
        in_specs=[pl.BlockSpec((1, 1, tc * TOP_K), lambda i: (0, 0, 0), memory_space=pltpu.SMEM),
                  pl.BlockSpec((1, 1, tc * TOP_K), lambda i: (jnp.minimum(i + 1, steps - 1), 0, 0),
                               memory_space=pltpu.SMEM),
                  pl.BlockSpec(memory_space=pl.ANY),
                  pl.BlockSpec((tc, LANES), row), pl.BlockSpec((tc, d), row), full(ln_g), full(ln_b)],
        out_specs=pl.BlockSpec((tc, d), row),
        out_shape=jax.ShapeDtypeStruct((t, d), F32),
        scratch_shapes=[pltpu.VMEM((2, TOP_K, tc * SUBLANES, LANES), F32), pltpu.SemaphoreType.DMA((2,))],
        compiler_params=_cparams(1),
        name="moe_combine_ln2",
    )(idx, idx, y_rows, gates, x, ln_g, ln_b)


def _routing_tables(top_idx, rank, counts):
    t = top_idx.shape[0]
    n_assign = t * TOP_K
    e_flat = top_idx.reshape(n_assign)
    padded = ((counts + MOE_BLOCK - 1) // MOE_BLOCK) * MOE_BLOCK
    pend = jnp.cumsum(padded)
    pstart = pend - padded
    dest = (pstart[e_flat] + rank.reshape(n_assign)).astype(jnp.int32)
    n_blocks = -(-n_assign // MOE_BLOCK) + N_EXPERTS
    tok_flat = jnp.arange(n_assign, dtype=jnp.int32) // TOP_K
    row_tok = jnp.zeros((n_blocks * MOE_BLOCK,), jnp.int32).at[dest].set(
        tok_flat, unique_indices=True, mode="promise_in_bounds")
    block_start = jnp.arange(n_blocks, dtype=jnp.int32) * MOE_BLOCK
    block_exp = jnp.minimum(jnp.sum(pend[None, :] <= block_start[:, None], axis=1), N_EXPERTS - 1).astype(jnp.int32)
    n_used = (pend[-1] // MOE_BLOCK).astype(jnp.int32).reshape(1)
    return dest, row_tok, block_exp, n_used


def _pad_rows(w, rows, at):
    out = jnp.zeros((rows, w.shape[1]), w.dtype)
    return out.at[at:at + w.shape[0]].set(w)


def _tile(n, pref):
    return pref if n % pref == 0 else n


def kernel(x, w_in, attn_sinks, w_attn_o, rwkv_mu, rwkv_w0, rwkv_w2, rwkv_a0, rwkv_a2, rwkv_g2, rwkv_k_k, rwkv_k_a, rwkv_r_k, rwkv_v0, rwkv_v1, rwkv_v2, rwkv_lnx_w, rwkv_lnx_b, w_rwkv_o, w_out, ln1_g, ln1_b, router_w, router_b, expert_w1, expert_b1, expert_w2, expert_b2, ln2_g, ln2_b):
    batch, seq, d = x.shape
    depth = w_in.shape[0]
    t = batch * seq
    alpha = (2.0 * depth) ** 0.25
    dr = RWKV_DIM
    tables = _rope_tables(seq)
    xf = x.reshape(t, d)
    v_first = None
    o_att, o_rw = ATT_Q + 2 * ATT_KV, ATT_Q + 2 * ATT_KV + 3 * dr + DECAY_LORA + AAA_LORA + GATE_LORA

    for l in range(depth):
        wl = w_in[l]
        wp = jnp.concatenate([
            wl[:, :ATT_Q], wl[:, o_att:o_att + 3 * dr], wl[:, o_rw:o_rw + 2 * d],
            wl[:, ATT_Q:o_att], wl[:, o_att + 3 * dr:o_rw]], axis=1).astype(BF16)
        p = _project(xf, wp, _tile(t, 512), P_COLS // 3)

        y_attn = _attention(p, attn_sinks[l], tables, w_attn_o[l].astype(BF16), batch, seq)

        mu = rwkv_mu[l]
        prm = {
            "mu_rkv": mu[:3 * dr].reshape(3, dr),
            "mu_lora": mu[3 * dr:].reshape(1, -1),
            "w0": rwkv_w0[l].reshape(1, dr), "a0": rwkv_a0[l].reshape(1, dr),
            "w2": _pad_rows(rwkv_w2[l], LANES, 0).astype(BF16),
            "a2": _pad_rows(rwkv_a2[l], LANES, DECAY_LORA).astype(BF16),
            "g2": rwkv_g2[l].astype(BF16),
        }
        if l > 0:
            prm["v0"] = rwkv_v0[l - 1].reshape(1, dr)
            prm["v1"] = jnp.zeros((dr, LANES), F32).at[:, :MV_LORA].set(rwkv_v1[l - 1]).astype(BF16)
            prm["v2"] = _pad_rows(rwkv_v2[l - 1], LANES, 0).astype(BF16)
        r_, k_, v_, lw_, a_, g_ = _rwkv_prep(p, prm, v_first, seq, _tile(seq, 256))
        if l == 0:
            v_first = v_
        scan_prm = jnp.stack([rwkv_k_k[l], rwkv_k_a[l], rwkv_r_k[l].reshape(dr), rwkv_lnx_w[l], rwkv_lnx_b[l],
                              jnp.zeros((dr,), F32), jnp.zeros((dr,), F32), jnp.zeros((dr,), F32)])
        z = _wkv_scan(r_, k_, v_, a_, g_, lw_, scan_prm, batch, seq)

        x1, x1_tiles = _merge(z, y_attn, p, xf, w_rwkv_o[l].astype(BF16), w_out[l].astype(BF16),
                    ln1_g[l].reshape(1, d), ln1_b[l].reshape(1, d), alpha, _tile(t, 256))

        rw = jnp.zeros((d, LANES), F32).at[:, :N_EXPERTS].set(router_w[l])
        rb = jnp.full((1, LANES), -1e30, F32).at[0, :N_EXPERTS].set(router_b[l])
        top_idx, gates, rank, counts = _router(x1, jnp.stack(_split3(rw)), rb, _tile(t, 512))
        dest, row_tok, block_exp, n_used = _routing_tables(top_idx[:, :TOP_K], rank[:, :TOP_K],
                                                           counts[0, :N_EXPERTS])

        b1 = expert_b1[l]
        w1g, w1l = _deinterleave(expert_w1[l].reshape(N_EXPERTS * d, 2 * D_EXPERT), 512)
        y_rows = _experts(x1_tiles, row_tok, block_exp, n_used,
                          w1g.reshape(N_EXPERTS, d, D_EXPERT), w1l.reshape(N_EXPERTS, d, D_EXPERT),
                          b1[:, None, 0::2], b1[:, None, 1::2],
                          expert_w2[l].astype(BF16), expert_b2[l][:, None, :])
        xf = _combine(y_rows, dest, gates, x1, ln2_g[l].reshape(1, d), ln2_b[l].reshape(1, d),
                      alpha, _tile(t, 128))
    return xf.reshape(batch, seq, d)
```

```python
import functools

import jax
import jax.numpy as jnp
from jax import lax
from jax.experimental import pallas as pl
from jax.experimental.pallas import tpu as pltpu

F32 = jnp.float32
BF16 = jnp.bfloat16

D_MODEL = 1024
HEAD_DIM = 64
N_Q_HEADS = 16
N_KV_HEADS = 4
WINDOW = 128
ATTN_BLOCK = 128
ROPE_THETA = 10000.0
ATT_Q = N_Q_HEADS * HEAD_DIM
ATT_KV = N_KV_HEADS * HEAD_DIM

RWKV_HEAD = 64
RWKV_HEADS = 16
RWKV_DIM = RWKV_HEADS * RWKV_HEAD
DECAY_LORA = 64
AAA_LORA = 64
MV_LORA = 32
GATE_LORA = 128
RWKV_GN_EPS = 64e-5

N_EXPERTS = 32
TOP_K = 4
D_EXPERT = 1024
SWIGLU_LIMIT = 7.0
SWIGLU_ALPHA = 1.702
MOE_BLOCK = 256

LN_EPS = 1e-5

LANES = 128
HEAD_PAIRS = RWKV_HEADS // 2
SCAN_CHUNK = 64
SCAN_CHUNKS_PER_STEP = 2
VMEM_LIMIT = 56 * 1024 * 1024

COL_Q, COL_R, COL_K, COL_V, COL_GA, COL_GB = 0, 1024, 2048, 3072, 4096, 5120
COL_AK, COL_AV, COL_LORA = 6144, 6400, 6656
P_COLS = 6912


def _cparams(n_axes):
    return pltpu.CompilerParams(dimension_semantics=("arbitrary",) * n_axes,
                                vmem_limit_bytes=VMEM_LIMIT)


def _dot(a, b):
    return jnp.dot(a, b, preferred_element_type=F32)


def _dot_nt(a, b):
    return lax.dot_general(a, b, (((1,), (1,)), ((), ())), preferred_element_type=F32)


def _dot_tn(a, b):
    return lax.dot_general(a, b, (((0,), (0,)), ((), ())), preferred_element_type=F32)


def _sigmoid(z):
    return 1.0 / (1.0 + jnp.exp(-z))


def _layer_norm(h, g, b):
    mu = jnp.mean(h, axis=-1, keepdims=True)
    c = h - mu
    var = jnp.mean(c * c, axis=-1, keepdims=True)
    return c * lax.rsqrt(var + LN_EPS) * g + b


def _proj_kernel(x_ref, w_ref, o_ref):
    o_ref[...] = _dot(x_ref[...].astype(BF16), w_ref[...]).astype(o_ref.dtype)


def _project(x, w, tm, tn):
    m, k = x.shape
    n = w.shape[1]
    return pl.pallas_call(
        _proj_kernel,
        grid=(n // tn, m // tm),
        in_specs=[pl.BlockSpec((tm, k), lambda j, i: (i, 0)),
                  pl.BlockSpec((k, tn), lambda j, i: (0, j))],
        out_specs=pl.BlockSpec((tm, tn), lambda j, i: (i, j)),
        out_shape=jax.ShapeDtypeStruct((m, n), BF16),
        compiler_params=_cparams(2),
        name="proj",
    )(x, w)


def _rope(x, c, s1, s2):
    return x * c + pltpu.roll(x, LANES - HEAD_DIM // 2, 1) * s1 + pltpu.roll(x, HEAD_DIM // 2, 1) * s2


def _attn_kernel(sink_ref, q_ref, kc_ref, kp_ref, vc_ref, vp_ref, tc_ref, tp_ref, wo_ref, o_ref):
    n = pl.program_id(1)
    blk = ATTN_BLOCK
    lane = lax.broadcasted_iota(jnp.int32, (1, LANES), 1)
    lo = lane < HEAD_DIM
    cq, s1q, s2q = tc_ref[0], tc_ref[1], tc_ref[2]
    cp, s1p, s2p = tp_ref[0], tp_ref[1], tp_ref[2]

    row = lax.broadcasted_iota(jnp.int32, (blk, 2 * blk), 0)
    col = lax.broadcasted_iota(jnp.int32, (blk, 2 * blk), 1)
    valid = (col > row) & (col <= row + WINDOW) & ((n > 0) | (col >= blk))

    def lo_hi(t, off):
        sw = pltpu.roll(t, HEAD_DIM, 1)
        t_lo = jnp.where(lo, t if off == 0 else sw, 0.0).astype(BF16)
        t_hi = jnp.where(lo, 0.0, sw if off == 0 else t).astype(BF16)
        return t_lo, t_hi

    kv_tiles = []
    for c in range(ATT_KV // LANES):
        sl = slice(c * LANES, (c + 1) * LANES)
        kc = _rope(kc_ref[:, sl].astype(F32), cq, s1q, s2q)
        kp = _rope(kp_ref[:, sl].astype(F32), cp, s1p, s2p)
        kt = jnp.concatenate([kp, kc], axis=0)
        vt = jnp.concatenate([vp_ref[:, sl].astype(F32), vc_ref[:, sl].astype(F32)], axis=0)
        kv_tiles.append((kt, vt))

    k_m, v_st = [], []
    for g in range(N_KV_HEADS):
        kt, vt = kv_tiles[g // 2]
        k_m.append(lo_hi(kt, g % 2))
        v_st.append(jnp.concatenate(lo_hi(vt, g % 2), axis=0))
    n_pairs = N_Q_HEADS // 2
    qp = [(_rope(q_ref[:, j * LANES:(j + 1) * LANES].astype(F32), cq, s1q, s2q) * (HEAD_DIM ** -0.5)).astype(BF16)
          for j in range(n_pairs)]
    scores = [_dot_nt(qp[hh // 2], k_m[hh // 4][hh % 2]) for hh in range(N_Q_HEADS)]
    probs = []
    for hh in range(N_Q_HEADS):
        s = jnp.where(valid, scores[hh], -1e30)
        sink = sink_ref[hh]
        m = jnp.maximum(jnp.max(s, axis=-1, keepdims=True), sink)
        p = jnp.exp(s - m)
        den = jnp.sum(p, axis=-1, keepdims=True) + jnp.exp(sink - m)
        probs.append((p / den).astype(BF16))
    outs = [_dot(jnp.concatenate([probs[2 * j], probs[2 * j + 1]], axis=1), v_st[j // 2]) for j in range(n_pairs)]
    att = jnp.concatenate(outs, axis=1).astype(BF16)
    o_ref[...] = _dot(att, wo_ref[...]).astype(o_ref.dtype)


def _attention(p, sinks, tables, w_o, batch, seq):
    nb = seq // ATTN_BLOCK
    blk = ATTN_BLOCK
    kcol, vcol = COL_AK // ATT_KV, COL_AV // ATT_KV
    cur = lambda b, n: b * nb + n
    prev = lambda b, n: b * nb + jnp.maximum(n - 1, 0)
    return pl.pallas_call(
        _attn_kernel,
        grid=(batch, nb),
        in_specs=[
            pl.BlockSpec(memory_space=pltpu.SMEM),
            pl.BlockSpec((blk, ATT_Q), lambda b, n: (cur(b, n), COL_Q // ATT_Q)),
            pl.BlockSpec((blk, ATT_KV), lambda b, n: (cur(b, n), kcol)),
            pl.BlockSpec((blk, ATT_KV), lambda b, n: (prev(b, n), kcol)),
            pl.BlockSpec((blk, ATT_KV), lambda b, n: (cur(b, n), vcol)),
            pl.BlockSpec((blk, ATT_KV), lambda b, n: (prev(b, n), vcol)),
            pl.BlockSpec((3, blk, LANES), lambda b, n: (0, n, 0)),
            pl.BlockSpec((3, blk, LANES), lambda b, n: (0, jnp.maximum(n - 1, 0), 0)),
            pl.BlockSpec((ATT_Q, D_MODEL), lambda b, n: (0, 0)),
        ],
        out_specs=pl.BlockSpec((blk, D_MODEL), lambda b, n: (cur(b, n), 0)),
        out_shape=jax.ShapeDtypeStruct((batch * seq, D_MODEL), BF16),
        compiler_params=_cparams(2),
        name="attn",
    )(sinks, p, p, p, p, p, tables, tables, w_o)


def _rope_tables(seq):
    pos = jnp.arange(seq, dtype=F32)
    inv = ROPE_THETA ** (-jnp.arange(0, HEAD_DIM, 2, dtype=F32) / HEAD_DIM)
    ang = pos[:, None] * inv[None, :]
    cos, sin = jnp.cos(ang), jnp.sin(ang)
    zero = jnp.zeros_like(sin)
    c = jnp.tile(cos, (1, LANES // (HEAD_DIM // 2)))
    s1 = jnp.tile(jnp.concatenate([-sin, zero], axis=1), (1, LANES // HEAD_DIM))
    s2 = jnp.tile(jnp.concatenate([zero, sin], axis=1), (1, LANES // HEAD_DIM))
    return jnp.stack([c, s1, s2])


def _scan_kernel(r_ref, k_ref, v_ref, a_ref, g_ref, lw_ref, prm_ref, o_ref, state_ref):
    ch = SCAN_CHUNK
    c_id = pl.program_id(1)

    @pl.when(c_id == 0)
    def _():
        state_ref[...] = jnp.zeros_like(state_ref)

    n_chunks = lw_ref.shape[0] // ch
    chunk_rows = [slice(cc * ch, (cc + 1) * ch) for cc in range(n_chunks)]
    tri = (lax.broadcasted_iota(jnp.int32, (ch, ch), 0) >= lax.broadcasted_iota(jnp.int32, (ch, ch), 1)
           ).astype(BF16)
    e_in, e_ex, e_inv, e_end, g_tot, e_ref = [], [], [], [], [], []
    for rows in chunk_rows:
        lw = lw_ref[rows, :]
        hi = lw.astype(BF16)
        rem = lw - hi.astype(F32)
        mid = rem.astype(BF16)
        low = (rem - mid.astype(F32)).astype(BF16)
        cum = _dot(tri, hi) + _dot(tri, mid) + _dot(tri, low)
        ref_row = cum[ch // 2 - 1:ch // 2, :]
        total = cum[ch - 1:ch, :]
        e_in.append(jnp.exp(cum - ref_row))
        e_ex.append(jnp.exp(cum - lw - ref_row))
        e_inv.append(jnp.exp(ref_row - cum))
        e_end.append(jnp.exp(total - cum))
        g_tot.append(jnp.exp(total))
        e_ref.append(jnp.exp(ref_row))

    lane = lax.broadcasted_iota(jnp.int32, (1, LANES), 1)
    lo = lane < RWKV_HEAD
    r2 = lax.broadcasted_iota(jnp.int32, (2 * ch, 2 * ch), 0)
    c2 = lax.broadcasted_iota(jnp.int32, (2 * ch, 2 * ch), 1)
    same = ((r2 < ch) & (c2 < ch)) | ((r2 >= ch) & (c2 >= ch))
    tri_strict = same & (r2 > c2)
    tri_incl = same & (r2 >= c2)
    eye = jnp.where(r2 == c2, 1.0, 0.0)
    rb = lax.broadcasted_iota(jnp.int32, (2 * ch, LANES), 0)
    cb = lax.broadcasted_iota(jnp.int32, (2 * ch, LANES), 1)
    blockmask = ((rb < ch) & (cb < RWKV_HEAD)) | ((rb >= ch) & (cb >= RWKV_HEAD))

    def stack(x):
        return jnp.concatenate([jnp.where(lo, x, 0.0), jnp.where(lo, 0.0, x)], axis=0)

    def head_sums(x):
        s_e = jnp.sum(jnp.where(lo, x, 0.0), axis=-1, keepdims=True)
        s_o = jnp.sum(jnp.where(lo, 0.0, x), axis=-1, keepdims=True)
        return jnp.where(lo, s_e, s_o)

    sls = [slice(pr * LANES, (pr + 1) * LANES) for pr in range(HEAD_PAIRS)]
    items = [(cc, pr) for cc in range(n_chunks) for pr in range(HEAD_PAIRS)]
    pairs = range(len(items))

    def kmod_of(rows, sl):
        return k_ref[rows, sl].astype(F32) * (1.0 + (a_ref[rows, sl].astype(F32) - 1.0) * prm_ref[1:2, sl])

    abar, rbar, vs, lhs, rhs, upd = [], [], [], [], [], []
    for cc, pr in items:
        rows, sl = chunk_rows[cc], sls[pr]
        a = a_ref[rows, sl].astype(F32)
        kx = k_ref[rows, sl].astype(F32) * prm_ref[0:1, sl]
        kk = kx / jnp.maximum(jnp.sqrt(head_sums(kx * kx)), 1e-12)
        beta = kk * a
        kmod = kmod_of(rows, sl)
        ab = stack(-kk * e_ex[cc][:, sl]).astype(BF16)
        rb_ = stack(r_ref[rows, sl].astype(F32) * e_in[cc][:, sl]).astype(BF16)
        abar.append(ab)
        rbar.append(rb_)
        lhs.append(jnp.concatenate([ab, rb_], axis=0))
        rhs.append(jnp.concatenate([stack(beta * e_inv[cc][:, sl]).astype(BF16),
                                    stack(kmod * e_inv[cc][:, sl]).astype(BF16)], axis=0))
        upd.append(jnp.concatenate([stack(beta * e_end[cc][:, sl]).astype(BF16),
                                    stack(kmod * e_end[cc][:, sl]).astype(BF16)], axis=0))
        vs.append(stack(v_ref[rows, sl].astype(F32)).astype(BF16))

    grams = [_dot_nt(lhs[pr], rhs[pr]) for pr in pairs]
    a_ak = [jnp.where(tri_strict, gm[:2 * ch, 2 * ch:], 0.0).astype(BF16) for gm in grams]
    m_r = [jnp.concatenate([jnp.where(tri_incl, gm[2 * ch:, :2 * ch], 0.0),
                            jnp.where(tri_incl, gm[2 * ch:, 2 * ch:], 0.0)], axis=1).astype(BF16) for gm in grams]

    a_ab = [jnp.where(tri_strict, gm[:2 * ch, :2 * ch], 0.0) for gm in grams]
    pw = [x.astype(BF16) for x in a_ab]
    pw = [_dot(x, x).astype(BF16) for x in pw]
    tinv = [eye + x for x in a_ab]
    for _ in range((ch - 1).bit_length() - 2):
        both = [_dot(pw[pr], jnp.concatenate([pw[pr], tinv[pr].astype(BF16)], axis=1)) for pr in pairs]
        tinv = [tinv[pr] + both[pr][:, 2 * ch:] for pr in pairs]
        pw = [both[pr][:, :2 * ch].astype(BF16) for pr in pairs]
    tinv = [tinv[pr] + _dot(pw[pr], tinv[pr].astype(BF16)) for pr in pairs]

    heads = range(HEAD_PAIRS)
    st = [state_ref[pr] for pr in heads]
    o_all = []
    for cc in range(n_chunks):
        it = [cc * HEAD_PAIRS + pr for pr in heads]
        st_b = [(st[pr] * e_ref[cc][:, sls[pr]]).astype(BF16) for pr in heads]
        w = [_dot_nt(abar[it[pr]], st_b[pr]) + _dot(a_ak[it[pr]], vs[it[pr]]) for pr in heads]
        u = [_dot(tinv[it[pr]].astype(BF16), w[pr].astype(BF16)) for pr in heads]
        uv = [jnp.concatenate([u[pr].astype(BF16), vs[it[pr]]], axis=0) for pr in heads]
        o_all.append([_dot_nt(rbar[it[pr]], st_b[pr]) + _dot(m_r[it[pr]], uv[pr]) for pr in heads])
        st = [st[pr] * g_tot[cc][:, sls[pr]] + _dot_tn(uv[pr], upd[it[pr]]) for pr in heads]
    for pr in heads:
        state_ref[pr] = st[pr]

    for cc, rows in enumerate(chunk_rows):
        for pr, sl in enumerate(sls):
            o_s = o_all[cc][pr]
            mean = jnp.sum(o_s, axis=-1, keepdims=True) * (1.0 / RWKV_HEAD)
            cen = jnp.where(blockmask, o_s - mean, 0.0)
            var = jnp.sum(cen * cen, axis=-1, keepdims=True) * (1.0 / RWKV_HEAD)
            gn_s = cen * lax.rsqrt(var + RWKV_GN_EPS)
            gn = gn_s[:ch] + gn_s[ch:]
            v = v_ref[rows, sl].astype(F32)
            bonus = head_sums(r_ref[rows, sl].astype(F32) * kmod_of(rows, sl) * prm_ref[2:3, sl]) * v
            z = (gn * prm_ref[3:4, sl] + prm_ref[4:5, sl] + bonus) * g_ref[rows, sl].astype(F32)
            o_ref[rows, sl] = z.astype(o_ref.dtype)


LORA_COLS = DECAY_LORA + AAA_LORA + GATE_LORA
CARRY_COLS = 3 * RWKV_DIM + LORA_COLS


def _rwkv_kernel(has_vres, *refs):
    if has_vres:
        (r_in, k_in, v_in, l_in, mu_ref, mul_ref, w0_ref, a0_ref, w2_ref, a2_ref, g2_ref,
         vf_ref, v0_ref, v1_ref, v2_ref, prm_ref,
         o_ref, vout_ref, state_ref, carry_ref, r_s, k_s, v_s, a_s, g_s, lw_s) = refs
    else:
        (r_in, k_in, v_in, l_in, mu_ref, mul_ref, w0_ref, a0_ref, w2_ref, a2_ref, g2_ref, prm_ref,
         o_ref, vout_ref, state_ref, carry_ref, r_s, k_s, v_s, a_s, g_s, lw_s) = refs
    rows = r_in.shape[0]
    dr = RWKV_DIM

    @pl.when(pl.program_id(1) == 0)
    def _():
        carry_ref[...] = jnp.zeros_like(carry_ref)

    row0 = lax.broadcasted_iota(jnp.int32, (rows, 1), 0) == 0

    def shift_mix(cur_ref, col, mu):
        cur = cur_ref[...].astype(F32)
        last = carry_ref[0:1, col:col + cur.shape[1]]
        sh = jnp.where(row0, last, pltpu.roll(cur, 1, 0))
        return cur + (sh - cur) * mu, cur[rows - 1:rows, :]

    r, r_last = shift_mix(r_in, 0, mu_ref[0:1, :])
    k, k_last = shift_mix(k_in, dr, mu_ref[1:2, :])
    v, v_last = shift_mix(v_in, 2 * dr, mu_ref[2:3, :])
    xl, l_last = shift_mix(l_in, 3 * dr, mul_ref[...])
    carry_ref[0:1, 0:dr] = r_last
    carry_ref[0:1, dr:2 * dr] = k_last
    carry_ref[0:1, 2 * dr:3 * dr] = v_last
    carry_ref[0:1, 3 * dr:] = l_last

    wa = xl[:, :LANES]
    zw = w0_ref[...] + _dot(jnp.tanh(wa).astype(BF16), w2_ref[...])
    lw_s[...] = -jnp.exp(-0.5) * _sigmoid(zw)
    a_s[...] = _sigmoid(a0_ref[...] + _dot(wa.astype(BF16), a2_ref[...]))
    g_s[...] = _dot(_sigmoid(xl[:, LANES:]).astype(BF16), g2_ref[...])
    if has_vres:
        lo_rank = _dot(v.astype(BF16), v1_ref[...])
        mix = _sigmoid(v0_ref[...] + _dot(lo_rank.astype(BF16), v2_ref[...]))
        v = v + (vf_ref[...].astype(F32) - v) * mix
    r_s[...] = r
    k_s[...] = k
    v_s[...] = v
    vout_ref[...] = v.astype(vout_ref.dtype)
    _scan_kernel(r_s, k_s, v_s, a_s, g_s, lw_s, prm_ref, o_ref, state_ref)


def _rwkv_mix(p, prm, scan_prm, v_first, batch, seq):
    rows = SCAN_CHUNK * SCAN_CHUNKS_PER_STEP
    nc = seq // rows
    dr = RWKV_DIM
    has_vres = v_first is not None
    full = lambda a: pl.BlockSpec(a.shape, lambda b, c: (0,) * a.ndim)
    col = lambda cb: (lambda b, c: (b * nc + c, cb))
    in_specs = [pl.BlockSpec((rows, dr), col(COL_R // dr)),
                pl.BlockSpec((rows, dr), col(COL_K // dr)),
                pl.BlockSpec((rows, dr), col(COL_V // dr)),
                pl.BlockSpec((rows, LORA_COLS), col(COL_LORA // LORA_COLS))]
    args = [p] * 4
    small = [prm["mu_rkv"], prm["mu_lora"], prm["w0"], prm["a0"], prm["w2"], prm["a2"], prm["g2"]]
    if has_vres:
        small_v = [prm["v0"], prm["v1"], prm["v2"]]
        in_specs += [full(a) for a in small] + [pl.BlockSpec((rows, dr), col(0))] + [full(a) for a in small_v]
        args += small + [v_first] + small_v
    else:
        in_specs += [full(a) for a in small]
        args += small
    in_specs.append(full(scan_prm))
    args.append(scan_prm)
    blk = pl.BlockSpec((rows, dr), col(0))
    out = jax.ShapeDtypeStruct((batch * seq, dr), BF16)
    return pl.pallas_call(
        functools.partial(_rwkv_kernel, has_vres),
        grid=(batch, nc),
        in_specs=in_specs,
        out_specs=[blk, blk],
        out_shape=[out, out],
        scratch_shapes=[pltpu.VMEM((HEAD_PAIRS, 2 * RWKV_HEAD, LANES), F32),
                        pltpu.VMEM((SUBLANES, CARRY_COLS), F32)] + [pltpu.VMEM((rows, dr), F32)] * 6,
        compiler_params=_cparams(2),
        name="rwkv_mix",
    )(*args)


def _merge_kernel(alpha, z_ref, ya_ref, ga_ref, gb_ref, x_ref, wr_ref, wo_ref, lg_ref, lb_ref, o_ref, ot_ref):
    y_r = _dot(z_ref[...], wr_ref[...])
    mix = _sigmoid(ga_ref[...].astype(F32)) * ya_ref[...].astype(F32) + _sigmoid(gb_ref[...].astype(F32)) * y_r
    mixed = _dot(mix.astype(BF16), wo_ref[...])
    out = _layer_norm(alpha * x_ref[...] + mixed, lg_ref[...], lb_ref[...])
    o_ref[...] = out
    _store_token_rows(ot_ref, out)


def _merge(z, y_attn, p, x, w_r, w_o, ln_g, ln_b, alpha, tm):
    t, d = x.shape
    row = lambda i: (i, 0)
    full = lambda a: pl.BlockSpec(a.shape, lambda i: (0,) * a.ndim)
    return pl.pallas_call(
        functools.partial(_merge_kernel, alpha),
        grid=(t // tm,),
        in_specs=[pl.BlockSpec((tm, d), row), pl.BlockSpec((tm, d), row),
                  pl.BlockSpec((tm, d), lambda i: (i, COL_GA // d)),
                  pl.BlockSpec((tm, d), lambda i: (i, COL_GB // d)),
                  pl.BlockSpec((tm, d), row), full(w_r), full(w_o), full(ln_g), full(ln_b)],
        out_specs=[pl.BlockSpec((tm, d), row), pl.BlockSpec((tm * SUBLANES, LANES), row)],
        out_shape=[jax.ShapeDtypeStruct((t, d), F32), jax.ShapeDtypeStruct((t * SUBLANES, LANES), F32)],
        compiler_params=_cparams(1),
        name="merge_ln1",
    )(z, y_attn, p, p, x, w_r, w_o, ln_g, ln_b)


def _split3(x):
    hi = x.astype(BF16)
    rem = x - hi.astype(F32)
    mid = rem.astype(BF16)
    low = (rem - mid.astype(F32)).astype(BF16)
    return hi, mid, low


def _router_kernel(x_ref, w_ref, b_ref, idx_ref, gate_ref, rank_ref, cnt_ref, carry_ref):
    @pl.when(pl.program_id(0) == 0)
    def _():
        carry_ref[...] = jnp.zeros_like(carry_ref)

    x0, x1, x2 = _split3(x_ref[...])
    w0, w1, w2 = w_ref[0], w_ref[1], w_ref[2]
    logits = (_dot(x0, w0) + (_dot(x0, w1) + _dot(x1, w0)) +
              (_dot(x0, w2) + _dot(x1, w1) + _dot(x2, w0))) + b_ref[...]
    tm = logits.shape[0]
    lane = lax.broadcasted_iota(jnp.int32, (tm, LANES), 1).astype(F32)
    work = logits
    idx_out = jnp.zeros((tm, LANES), F32)
    val_out = jnp.zeros((tm, LANES), F32)
    top = None
    den = jnp.zeros((tm, 1), F32)
    hits = []
    for kk in range(TOP_K):
        m = jnp.max(work, axis=-1, keepdims=True)
        sel = jnp.min(jnp.where(work == m, lane, float(LANES)), axis=-1, keepdims=True)
        if kk == 0:
            top = m
        e = jnp.exp(m - top)
        den = den + e
        idx_out = jnp.where(lane == float(kk), sel, idx_out)
        val_out = jnp.where(lane == float(kk), e, val_out)
        hits.append(lane == sel)
        work = jnp.where(hits[-1], -jnp.inf, work)
    idx_ref[...] = idx_out.astype(jnp.int32)
    gate_ref[...] = val_out / den

    chosen = jnp.where(hits[0] | hits[1] | hits[2] | hits[3], 1.0, 0.0)
    r_i = lax.broadcasted_iota(jnp.int32, (tm, tm), 0)
    c_i = lax.broadcasted_iota(jnp.int32, (tm, tm), 1)
    earlier = jnp.where(r_i > c_i, 1.0, 0.0).astype(BF16)
    before = carry_ref[0:1, :] + _dot(earlier, chosen.astype(BF16))
    rank_out = jnp.zeros((tm, LANES), F32)
    for kk in range(TOP_K):
        rk = jnp.sum(jnp.where(hits[kk], before, 0.0), axis=-1, keepdims=True)
        rank_out = jnp.where(lane == float(kk), rk, rank_out)
    rank_ref[...] = rank_out.astype(jnp.int32)
    carry_ref[0:1, :] = carry_ref[0:1, :] + jnp.sum(chosen, axis=0, keepdims=True)
    cnt_ref[...] = carry_ref[...].astype(jnp.int32)


def _router(x, w3, b, tm):
    t, d = x.shape
    row = lambda i: (i, 0)
    return pl.pallas_call(
        _router_kernel,
        grid=(t // tm,),
        in_specs=[pl.BlockSpec((tm, d), row), pl.BlockSpec(w3.shape, lambda i: (0, 0, 0)),
                  pl.BlockSpec(b.shape, lambda i: (0, 0))],
        out_specs=[pl.BlockSpec((tm, LANES), row)] * 3 + [pl.BlockSpec((8, LANES), lambda i: (0, 0))],
        out_shape=[jax.ShapeDtypeStruct((t, LANES), jnp.int32), jax.ShapeDtypeStruct((t, LANES), F32),
                   jax.ShapeDtypeStruct((t, LANES), jnp.int32), jax.ShapeDtypeStruct((8, LANES), jnp.int32)],
        scratch_shapes=[pltpu.VMEM((8, LANES), F32)],
        compiler_params=_cparams(1),
        name="router",
    )(x, w3, b)


SUBLANES = 8
TOKEN_TILES = D_MODEL // LANES


def _row_copy(src_hbm, dst_vmem, sem, src_row8, dst_tok):
    if not isinstance(src_row8, int):
        src_row8 = pl.multiple_of(src_row8, SUBLANES)
    return pltpu.make_async_copy(src_hbm.at[pl.ds(src_row8, SUBLANES), :],
                                 dst_vmem.at[pl.ds(dst_tok * SUBLANES, SUBLANES), :], sem)


def _load_token_rows(ref, n):
    return jnp.concatenate([ref[pl.ds(s, n, stride=SUBLANES), :] for s in range(TOKEN_TILES)], axis=1)


def _store_token_rows(ref, val):
    n = val.shape[0]
    for s in range(TOKEN_TILES):
        ref[pl.ds(s, n, stride=SUBLANES), :] = val[:, s * LANES:(s + 1) * LANES]


DEINT_GROUP = 2 * LANES


def _deint_kernel(w_ref, sel_ref, g_ref, l_ref):
    for c in range(w_ref.shape[1] // DEINT_GROUP):
        blk = w_ref[:, c * DEINT_GROUP:(c + 1) * DEINT_GROUP].astype(BF16)
        both = _dot(blk, sel_ref[...])
        g_ref[:, c * LANES:(c + 1) * LANES] = both[:, :LANES].astype(BF16)
        l_ref[:, c * LANES:(c + 1) * LANES] = both[:, LANES:].astype(BF16)


def _deinterleave(w, tm):
    rows, cols = w.shape
    src = jnp.arange(DEINT_GROUP)[:, None]
    dst = jnp.arange(DEINT_GROUP)[None, :]
    sel = jnp.where(dst < LANES, src == 2 * dst, src == 2 * (dst - LANES) + 1).astype(BF16)
    half = jax.ShapeDtypeStruct((rows, cols // 2), BF16)
    return pl.pallas_call(
        _deint_kernel,
        grid=(rows // tm,),
        in_specs=[pl.BlockSpec((tm, cols), lambda i: (i, 0)), pl.BlockSpec(sel.shape, lambda i: (0, 0))],
        out_specs=[pl.BlockSpec((tm, cols // 2), lambda i: (i, 0))] * 2,
        out_shape=[half, half],
        compiler_params=_cparams(1),
        name="deinterleave",
    )(w, sel)


def _expert_kernel(be_ref, nu_ref, idx0_ref, idxn_ref, x_hbm, w1g_ref, w1l_ref, b1g_ref, b1l_ref, w2_ref, b2_ref,
                   o_ref, xbuf, sems):
    i = pl.program_id(0)
    last = pl.num_programs(0) - 1
    slot = lax.rem(i, 2)
    used = i < nu_ref[0]

    def issue(idx_ref, s, start=0, stop=MOE_BLOCK):
        for r in range(start, stop):
            _row_copy(x_hbm, xbuf.at[s], sems.at[s], idx_ref[0, 0, r], r).start(priority=r % 2)

    def drain(s):
        for r in range(MOE_BLOCK):
            _row_copy(x_hbm, xbuf.at[s], sems.at[s], 0, r).wait()

    @pl.when(i == 0)
    def _():
        issue(idx0_ref, 0)

    @pl.when(jnp.logical_not(used))
    def _():
        issue(idxn_ref, 1 - slot)
        drain(slot)
        o_ref[...] = jnp.zeros_like(o_ref)

    @pl.when(used)
    def _():
        q = MOE_BLOCK // 4
        issue(idxn_ref, 1 - slot, 0, q)
        drain(slot)
        xb = _load_token_rows(xbuf.at[slot], MOE_BLOCK).astype(BF16)
        h_glu = _dot(xb, w1g_ref[...]) + b1g_ref[...]
        issue(idxn_ref, 1 - slot, q, 2 * q)
        h_lin = _dot(xb, w1l_ref[...]) + b1l_ref[...]
        issue(idxn_ref, 1 - slot, 2 * q, 3 * q)
        x_glu = jnp.minimum(h_glu, SWIGLU_LIMIT)
        x_lin = jnp.clip(h_lin, -SWIGLU_LIMIT, SWIGLU_LIMIT)
        act = x_glu * _sigmoid(SWIGLU_ALPHA * x_glu) * (x_lin + 1.0)
        issue(idxn_ref, 1 - slot, 3 * q, MOE_BLOCK)
        _store_token_rows(o_ref, _dot(act.astype(BF16), w2_ref[...]) + b2_ref[...])

    @pl.when(i == last)
    def _():
        drain(1 - slot)


def _experts(x_tiles, row_tok, block_exp, n_used, w1g, w1l, b1g, b1l, w2, b2):
    d = D_MODEL
    nblk = row_tok.shape[0] // MOE_BLOCK
    de = w1g.shape[2]
    idx = (row_tok * SUBLANES).reshape(nblk, 1, MOE_BLOCK)
    blk_rows = MOE_BLOCK * SUBLANES
    wmap = lambda i, be, nu: (be[i], 0, 0)
    grid_spec = pltpu.PrefetchScalarGridSpec(
        num_scalar_prefetch=2,
        grid=(nblk,),
        in_specs=[pl.BlockSpec((1, 1, MOE_BLOCK), lambda i, be, nu: (0, 0, 0), memory_space=pltpu.SMEM),
                  pl.BlockSpec((1, 1, MOE_BLOCK), lambda i, be, nu: (jnp.minimum(i + 1, nblk - 1), 0, 0),
                               memory_space=pltpu.SMEM),
                  pl.BlockSpec(memory_space=pl.ANY),
                  pl.BlockSpec((None, d, de), wmap), pl.BlockSpec((None, d, de), wmap),
                  pl.BlockSpec((None, 1, de), wmap), pl.BlockSpec((None, 1, de), wmap),
                  pl.BlockSpec((None, de, d), wmap), pl.BlockSpec((None, 1, d), wmap)],
        out_specs=pl.BlockSpec((blk_rows, LANES), lambda i, be, nu: (i, 0)),
        scratch_shapes=[pltpu.VMEM((2, blk_rows, LANES), F32), pltpu.SemaphoreType.DMA((2,))],
    )
    return pl.pallas_call(
        _expert_kernel,
        grid_spec=grid_spec,
        out_shape=jax.ShapeDtypeStruct((nblk * blk_rows, LANES), F32),
        compiler_params=_cparams(1),
        name="moe_experts",
    )(block_exp, n_used, idx, idx, x_tiles, w1g, w1l, b1g, b1l, w2, b2)


def _combine_kernel(alpha, idx0_ref, idxn_ref, y_hbm, gate_ref, x_ref, lg_ref, lb_ref, o_ref, buf, sems):
    tc = x_ref.shape[0]
    i = pl.program_id(0)
    last = pl.num_programs(0) - 1
    slot = lax.rem(i, 2)

    def issue(idx_ref, s):
        for r in range(tc):
            for kk in range(TOP_K):
                _row_copy(y_hbm, buf.at[s, kk], sems.at[s], idx_ref[0, 0, r * TOP_K + kk], r).start(priority=kk % 2)

    def drain(s):
        for r in range(tc * TOP_K):
            _row_copy(y_hbm, buf.at[s, 0], sems.at[s], 0, 0).wait()

    @pl.when(i == 0)
    def _():
        issue(idx0_ref, 0)

    issue(idxn_ref, 1 - slot)
    drain(slot)

    @pl.when(i == last)
    def _():
        drain(1 - slot)

    gates = gate_ref[...]
    y = _load_token_rows(buf.at[slot, 0], tc) * gates[:, 0:1]
    for kk in range(1, TOP_K):
        y = y + _load_token_rows(buf.at[slot, kk], tc) * gates[:, kk:kk + 1]
    o_ref[...] = _layer_norm(alpha * x_ref[...] + y, lg_ref[...], lb_ref[...])


def _combine(y_rows, dest, gates, x, ln_g, ln_b, alpha, tc):
    t, d = x.shape
    steps = t // tc
    row = lambda i: (i, 0)
    full = lambda a: pl.BlockSpec(a.shape, lambda i: (0,) * a.ndim)
    idx = (dest * SUBLANES).reshape(steps, 1, tc * TOP_K)
    return pl.pallas_call(
        functools.partial(_combine_kernel, alpha),
        grid=(steps,),
        in_specs=[pl.BlockSpec((1, 1, tc * TOP_K), lambda i: (0, 0, 0), memory_space=pltpu.SMEM),
                  pl.BlockSpec((1, 1, tc * TOP_K), lambda i: (jnp.minimum(i + 1, steps - 1), 0, 0),
                               memory_space=pltpu.SMEM),
                  pl.BlockSpec(memory_space=pl.ANY),
                  pl.BlockSpec((tc, LANES), row), pl.BlockSpec((tc, d), row), full(ln_g), full(ln_b)],
        out_specs=pl.BlockSpec((tc, d), row),
        out_shape=jax.ShapeDtypeStruct((t, d), F32),
        scratch_shapes=[pltpu.VMEM((2, TOP_K, tc * SUBLANES, LANES), F32), pltpu.SemaphoreType.DMA((2,))],
        compiler_params=_cparams(1),
        name="moe_combine_ln2",
    )(idx, idx, y_rows, gates, x, ln_g, ln_b)


def _routing_tables(top_idx, rank, counts):
    t = top_idx.shape[0]
    n_assign = t * TOP_K
    e_flat = top_idx.reshape(n_assign)
    padded = ((counts + MOE_BLOCK - 1) // MOE_BLOCK) * MOE_BLOCK
    pend = jnp.cumsum(padded)
    pstart = pend - padded
    dest = (pstart[e_flat] + rank.reshape(n_assign)).astype(jnp.int32)
    n_blocks = -(-n_assign // MOE_BLOCK) + N_EXPERTS
    tok_flat = jnp.arange(n_assign, dtype=jnp.int32) // TOP_K
    row_tok = jnp.zeros((n_blocks * MOE_BLOCK,), jnp.int32).at[dest].set(
        tok_flat, unique_indices=True, mode="promise_in_bounds")
    block_start = jnp.arange(n_blocks, dtype=jnp.int32) * MOE_BLOCK
    block_exp = jnp.minimum(jnp.sum(pend[None, :] <= block_start[:, None], axis=1), N_EXPERTS - 1).astype(jnp.int32)
    n_used = (pend[-1] // MOE_BLOCK).astype(jnp.int32).reshape(1)
    return dest, row_tok, block_exp, n_used


def _pad_rows(w, rows, at):
    out = jnp.zeros((rows, w.shape[1]), w.dtype)
    return out.at[at:at + w.shape[0]].set(w)


def _tile(n, pref):
    return pref if n % pref == 0 else n


def kernel(x, w_in, attn_sinks, w_attn_o, rwkv_mu, rwkv_w0, rwkv_w2, rwkv_a0, rwkv_a2, rwkv_g2, rwkv_k_k, rwkv_k_a, rwkv_r_k, rwkv_v0, rwkv_v1, rwkv_v2, rwkv_lnx_w, rwkv_lnx_b, w_rwkv_o, w_out, ln1_g, ln1_b, router_w, router_b, expert_w1, expert_b1, expert_w2, expert_b2, ln2_g, ln2_b):
    batch, seq, d = x.shape
    depth = w_in.shape[0]
    t = batch * seq
    alpha = (2.0 * depth) ** 0.25
    dr = RWKV_DIM
    tables = _rope_tables(seq)
    xf = x.reshape(t, d)
    v_first = None
    o_att, o_rw = ATT_Q + 2 * ATT_KV, ATT_Q + 2 * ATT_KV + 3 * dr + DECAY_LORA + AAA_LORA + GATE_LORA

    for l in range(depth):
        wl = w_in[l]
        wp = jnp.concatenate([
            wl[:, :ATT_Q], wl[:, o_att:o_att + 3 * dr], wl[:, o_rw:o_rw + 2 * d],
            wl[:, ATT_Q:o_att], wl[:, o_att + 3 * dr:o_rw]], axis=1).astype(BF16)
        p = _project(xf, wp, _tile(t, 512), P_COLS // 3)

        y_attn = _attention(p, attn_sinks[l], tables, w_attn_o[l].astype(BF16), batch, seq)

        mu = rwkv_mu[l]
        prm = {
            "mu_rkv": mu[:3 * dr].reshape(3, dr),
            "mu_lora": mu[3 * dr:].reshape(1, -1),
            "w0": rwkv_w0[l].reshape(1, dr), "a0": rwkv_a0[l].reshape(1, dr),
            "w2": _pad_rows(rwkv_w2[l], LANES, 0).astype(BF16),
            "a2": _pad_rows(rwkv_a2[l], LANES, DECAY_LORA).astype(BF16),
            "g2": rwkv_g2[l].astype(BF16),
        }
        if l > 0:
            prm["v0"] = rwkv_v0[l - 1].reshape(1, dr)
            prm["v1"] = jnp.zeros((dr, LANES), F32).at[:, :MV_LORA].set(rwkv_v1[l - 1]).astype(BF16)
            prm["v2"] = _pad_rows(rwkv_v2[l - 1], LANES, 0).astype(BF16)
        scan_prm = jnp.stack([rwkv_k_k[l], rwkv_k_a[l], rwkv_r_k[l].reshape(dr), rwkv_lnx_w[l], rwkv_lnx_b[l],
                              jnp.zeros((dr,), F32), jnp.zeros((dr,), F32), jnp.zeros((dr,), F32)])
        z, v_ = _rwkv_mix(p, prm, scan_prm, v_first, batch, seq)
        if l == 0:
            v_first = v_

        x1, x1_tiles = _merge(z, y_attn, p, xf, w_rwkv_o[l].astype(BF16), w_out[l].astype(BF16),
                    ln1_g[l].reshape(1, d), ln1_b[l].reshape(1, d), alpha, _tile(t, 256))

        rw = jnp.zeros((d, LANES), F32).at[:, :N_EXPERTS].set(router_w[l])
        rb = jnp.full((1, LANES), -1e30, F32).at[0, :N_EXPERTS].set(router_b[l])
        top_idx, gates, rank, counts = _router(x1, jnp.stack(_split3(rw)), rb, _tile(t, 512))
        dest, row_tok, block_exp, n_used = _routing_tables(top_idx[:, :TOP_K], rank[:, :TOP_K],
                                                           counts[0, :N_EXPERTS])

        b1 = expert_b1[l]
        w1g, w1l = _deinterleave(expert_w1[l].reshape(N_EXPERTS * d, 2 * D_EXPERT), 512)
        y_rows = _experts(x1_tiles, row_tok, block_exp, n_used,
                          w1g.reshape(N_EXPERTS, d, D_EXPERT), w1l.reshape(N_EXPERTS, d, D_EXPERT),
                          b1[:, None, 0::2], b1[:, None, 1::2],
                          expert_w2[l].astype(BF16), expert_b2[l][:, None, :])
        xf = _combine(y_rows, dest, gates, x1, ln2_g[l].reshape(1, d), ln2_b[l].reshape(1, d),
                      alpha, _tile(t, 128))
    return xf.reshape(batch, seq, d)
```

```python
import functools

import jax
import jax.numpy as jnp
from jax import lax
from jax.experimental import pallas as pl
from jax.experimental.pallas import tpu as pltpu

F32 = jnp.float32
BF16 = jnp.bfloat16

D_MODEL = 1024
HEAD_DIM = 64
N_Q_HEADS = 16
N_KV_HEADS = 4
WINDOW = 128
ATTN_BLOCK = 128
ROPE_THETA = 10000.0
ATT_Q = N_Q_HEADS * HEAD_DIM
ATT_KV = N_KV_HEADS * HEAD_DIM

RWKV_HEAD = 64
RWKV_HEADS = 16
RWKV_DIM = RWKV_HEADS * RWKV_HEAD
DECAY_LORA = 64
AAA_LORA = 64
MV_LORA = 32
GATE_LORA = 128
RWKV_GN_EPS = 64e-5

N_EXPERTS = 32
TOP_K = 4
D_EXPERT = 1024
SWIGLU_LIMIT = 7.0
SWIGLU_ALPHA = 1.702
MOE_BLOCK = 256

LN_EPS = 1e-5

LANES = 128
HEAD_PAIRS = RWKV_HEADS // 2
SCAN_CHUNK = 64
SCAN_CHUNKS_PER_STEP = 2
VMEM_LIMIT = 56 * 1024 * 1024

COL_Q, COL_R, COL_K, COL_V, COL_GA, COL_GB = 0, 1024, 2048, 3072, 4096, 5120
COL_AK, COL_AV, COL_LORA = 6144, 6400, 6656
P_COLS = 6912


def _cparams(n_axes):
    return pltpu.CompilerParams(dimension_semantics=("arbitrary",) * n_axes,
                                vmem_limit_bytes=VMEM_LIMIT)


def _dot(a, b):
    return jnp.dot(a, b, preferred_element_type=F32)


def _dot_nt(a, b):
    return lax.dot_general(a, b, (((1,), (1,)), ((), ())), preferred_element_type=F32)


def _dot_tn(a, b):
    return lax.dot_general(a, b, (((0,), (0,)), ((), ())), preferred_element_type=F32)


def _sigmoid(z):
    return 1.0 / (1.0 + jnp.exp(-z))


def _layer_norm(h, g, b):
    mu = jnp.mean(h, axis=-1, keepdims=True)
    c = h - mu
    var = jnp.mean(c * c, axis=-1, keepdims=True)
    return c * lax.rsqrt(var + LN_EPS) * g + b


def _proj_kernel(x_ref, w_ref, o_ref):
    o_ref[...] = _dot(x_ref[...].astype(BF16), w_ref[...]).astype(o_ref.dtype)


def _project(x, w, tm, tn):
    m, k = x.shape
    n = w.shape[1]
    return pl.pallas_call(
        _proj_kernel,
        grid=(n // tn, m // tm),
        in_specs=[pl.BlockSpec((tm, k), lambda j, i: (i, 0)),
                  pl.BlockSpec((k, tn), lambda j, i: (0, j))],
        out_specs=pl.BlockSpec((tm, tn), lambda j, i: (i, j)),
        out_shape=jax.ShapeDtypeStruct((m, n), BF16),
        compiler_params=_cparams(2),
        name="proj",
    )(x, w)


def _rope(x, c, s1, s2):
    return x * c + pltpu.roll(x, LANES - HEAD_DIM // 2, 1) * s1 + pltpu.roll(x, HEAD_DIM // 2, 1) * s2


def _attn_kernel(sink_ref, q_ref, kc_ref, kp_ref, vc_ref, vp_ref, tc_ref, tp_ref, wo_ref, o_ref):
    n = pl.program_id(1)
    blk = ATTN_BLOCK
    lane = lax.broadcasted_iota(jnp.int32, (1, LANES), 1)
    lo = lane < HEAD_DIM
    cq, s1q, s2q = tc_ref[0], tc_ref[1], tc_ref[2]
    cp, s1p, s2p = tp_ref[0], tp_ref[1], tp_ref[2]

    row = lax.broadcasted_iota(jnp.int32, (blk, 2 * blk), 0)
    col = lax.broadcasted_iota(jnp.int32, (blk, 2 * blk), 1)
    valid = (col > row) & (col <= row + WINDOW) & ((n > 0) | (col >= blk))

    def lo_hi(t, off):
        sw = pltpu.roll(t, HEAD_DIM, 1)
        t_lo = jnp.where(lo, t if off == 0 else sw, 0.0).astype(BF16)
        t_hi = jnp.where(lo, 0.0, sw if off == 0 else t).astype(BF16)
        return t_lo, t_hi

    kv_tiles = []
    for c in range(ATT_KV // LANES):
        sl = slice(c * LANES, (c + 1) * LANES)
        kc = _rope(kc_ref[:, sl].astype(F32), cq, s1q, s2q)
        kp = _rope(kp_ref[:, sl].astype(F32), cp, s1p, s2p)
        kt = jnp.concatenate([kp, kc], axis=0)
        vt = jnp.concatenate([vp_ref[:, sl].astype(F32), vc_ref[:, sl].astype(F32)], axis=0)
        kv_tiles.append((kt, vt))

    k_m, v_st = [], []
    for g in range(N_KV_HEADS):
        kt, vt = kv_tiles[g // 2]
        k_m.append(lo_hi(kt, g % 2))
        v_st.append(jnp.concatenate(lo_hi(vt, g % 2), axis=0))
    n_pairs = N_Q_HEADS // 2
    log2e = 1.4426950408889634
    qp = [(_rope(q_ref[:, j * LANES:(j + 1) * LANES].astype(F32), cq, s1q, s2q)
           * (HEAD_DIM ** -0.5 * log2e)).astype(BF16) for j in range(n_pairs)]
    scores = [_dot_nt(qp[hh // 2], k_m[hh // 4][hh % 2]) for hh in range(N_Q_HEADS)]
    probs, rden = [], []
    for hh in range(N_Q_HEADS):
        s = jnp.where(valid, scores[hh], -1e30)
        sink = sink_ref[hh] * log2e
        m = jnp.maximum(jnp.max(s, axis=-1, keepdims=True), sink)
        p = jnp.exp2(s - m)
        rden.append(1.0 / (jnp.sum(p, axis=-1, keepdims=True) + jnp.exp2(sink - m)))
        probs.append(p.astype(BF16))
    outs = [_dot(jnp.concatenate([probs[2 * j], probs[2 * j + 1]], axis=1), v_st[j // 2])
            * jnp.where(lo, rden[2 * j], rden[2 * j + 1]) for j in range(n_pairs)]
    att = jnp.concatenate(outs, axis=1).astype(BF16)
    o_ref[...] = _dot(att, wo_ref[...]).astype(o_ref.dtype)


def _attention(p, sinks, tables, w_o, batch, seq):
    nb = seq // ATTN_BLOCK
    blk = ATTN_BLOCK
    kcol, vcol = COL_AK // ATT_KV, COL_AV // ATT_KV
    cur = lambda b, n: b * nb + n
    prev = lambda b, n: b * nb + jnp.maximum(n - 1, 0)
    return pl.pallas_call(
        _attn_kernel,
        grid=(batch, nb),
        in_specs=[
            pl.BlockSpec(memory_space=pltpu.SMEM),
            pl.BlockSpec((blk, ATT_Q), lambda b, n: (cur(b, n), COL_Q // ATT_Q)),
            pl.BlockSpec((blk, ATT_KV), lambda b, n: (cur(b, n), kcol)),
            pl.BlockSpec((blk, ATT_KV), lambda b, n: (prev(b, n), kcol)),
            pl.BlockSpec((blk, ATT_KV), lambda b, n: (cur(b, n), vcol)),
            pl.BlockSpec((blk, ATT_KV), lambda b, n: (prev(b, n), vcol)),
            pl.BlockSpec((3, blk, LANES), lambda b, n: (0, n, 0)),
            pl.BlockSpec((3, blk, LANES), lambda b, n: (0, jnp.maximum(n - 1, 0), 0)),
            pl.BlockSpec((ATT_Q, D_MODEL), lambda b, n: (0, 0)),
        ],
        out_specs=pl.BlockSpec((blk, D_MODEL), lambda b, n: (cur(b, n), 0)),
        out_shape=jax.ShapeDtypeStruct((batch * seq, D_MODEL), BF16),
        compiler_params=_cparams(2),
        name="attn",
    )(sinks, p, p, p, p, p, tables, tables, w_o)


def _rope_tables(seq):
    pos = jnp.arange(seq, dtype=F32)
    inv = ROPE_THETA ** (-jnp.arange(0, HEAD_DIM, 2, dtype=F32) / HEAD_DIM)
    ang = pos[:, None] * inv[None, :]
    cos, sin = jnp.cos(ang), jnp.sin(ang)
    zero = jnp.zeros_like(sin)
    c = jnp.tile(cos, (1, LANES // (HEAD_DIM // 2)))
    s1 = jnp.tile(jnp.concatenate([-sin, zero], axis=1), (1, LANES // HEAD_DIM))
    s2 = jnp.tile(jnp.concatenate([zero, sin], axis=1), (1, LANES // HEAD_DIM))
    return jnp.stack([c, s1, s2])


def _scan_kernel(r_ref, k_ref, v_ref, a_ref, g_ref, lw_ref, prm_ref, o_ref, state_ref):
    ch = SCAN_CHUNK
    c_id = pl.program_id(1)

    @pl.when(c_id == 0)
    def _():
        state_ref[...] = jnp.zeros_like(state_ref)

    n_chunks = lw_ref.shape[0] // ch
    chunk_rows = [slice(cc * ch, (cc + 1) * ch) for cc in range(n_chunks)]
    tri = (lax.broadcasted_iota(jnp.int32, (ch, ch), 0) >= lax.broadcasted_iota(jnp.int32, (ch, ch), 1)
           ).astype(BF16)
    e_in, e_ex, e_inv, e_end, g_tot, e_ref = [], [], [], [], [], []
    for rows in chunk_rows:
        lw = lw_ref[rows, :]
        hi = lw.astype(BF16)
        rem = lw - hi.astype(F32)
        mid = rem.astype(BF16)
        low = (rem - mid.astype(F32)).astype(BF16)
        cum = _dot(tri, hi) + _dot(tri, mid) + _dot(tri, low)
        ref_row = cum[ch // 2 - 1:ch // 2, :]
        total = cum[ch - 1:ch, :]
        e_in.append(jnp.exp(cum - ref_row))
        e_ex.append(jnp.exp(cum - lw - ref_row))
        e_inv.append(jnp.exp(ref_row - cum))
        e_end.append(jnp.exp(total - cum))
        g_tot.append(jnp.exp(total))
        e_ref.append(jnp.exp(ref_row))

    lane = lax.broadcasted_iota(jnp.int32, (1, LANES), 1)
    lo = lane < RWKV_HEAD
    r2 = lax.broadcasted_iota(jnp.int32, (2 * ch, 2 * ch), 0)
    c2 = lax.broadcasted_iota(jnp.int32, (2 * ch, 2 * ch), 1)
    same = ((r2 < ch) & (c2 < ch)) | ((r2 >= ch) & (c2 >= ch))
    tri_strict = same & (r2 > c2)
    tri_incl = same & (r2 >= c2)
    eye = jnp.where(r2 == c2, 1.0, 0.0)
    rb = lax.broadcasted_iota(jnp.int32, (2 * ch, LANES), 0)
    cb = lax.broadcasted_iota(jnp.int32, (2 * ch, LANES), 1)
    blockmask = ((rb < ch) & (cb < RWKV_HEAD)) | ((rb >= ch) & (cb >= RWKV_HEAD))

    def stack(x):
        return jnp.concatenate([jnp.where(lo, x, 0.0), jnp.where(lo, 0.0, x)], axis=0)

    def head_sums(x):
        s_e = jnp.sum(jnp.where(lo, x, 0.0), axis=-1, keepdims=True)
        s_o = jnp.sum(jnp.where(lo, 0.0, x), axis=-1, keepdims=True)
        return jnp.where(lo, s_e, s_o)

    sls = [slice(pr * LANES, (pr + 1) * LANES) for pr in range(HEAD_PAIRS)]
    items = [(cc, pr) for cc in range(n_chunks) for pr in range(HEAD_PAIRS)]
    pairs = range(len(items))

    def kmod_of(rows, sl):
        return k_ref[rows, sl].astype(F32) * (1.0 + (a_ref[rows, sl].astype(F32) - 1.0) * prm_ref[1:2, sl])

    abar, rbar, vs, lhs, rhs, upd = [], [], [], [], [], []
    for cc, pr in items:
        rows, sl = chunk_rows[cc], sls[pr]
        a = a_ref[rows, sl].astype(F32)
        kx = k_ref[rows, sl].astype(F32) * prm_ref[0:1, sl]
        kk = kx / jnp.maximum(jnp.sqrt(head_sums(kx * kx)), 1e-12)
        beta = kk * a
        kmod = kmod_of(rows, sl)
        ab = stack(-kk * e_ex[cc][:, sl]).astype(BF16)
        rb_ = stack(r_ref[rows, sl].astype(F32) * e_in[cc][:, sl]).astype(BF16)
        abar.append(ab)
        rbar.append(rb_)
        lhs.append(jnp.concatenate([ab, rb_], axis=0))
        rhs.append(jnp.concatenate([stack(beta * e_inv[cc][:, sl]).astype(BF16),
                                    stack(kmod * e_inv[cc][:, sl]).astype(BF16)], axis=0))
        upd.append(jnp.concatenate([stack(beta * e_end[cc][:, sl]).astype(BF16),
                                    stack(kmod * e_end[cc][:, sl]).astype(BF16)], axis=0))
        vs.append(stack(v_ref[rows, sl].astype(F32)).astype(BF16))

    grams = [_dot_nt(lhs[pr], rhs[pr]) for pr in pairs]
    a_ak = [jnp.where(tri_strict, gm[:2 * ch, 2 * ch:], 0.0).astype(BF16) for gm in grams]
    m_r = [jnp.concatenate([jnp.where(tri_incl, gm[2 * ch:, :2 * ch], 0.0),
                            jnp.where(tri_incl, gm[2 * ch:, 2 * ch:], 0.0)], axis=1).astype(BF16) for gm in grams]

    a_ab = [jnp.where(tri_strict, gm[:2 * ch, :2 * ch], 0.0) for gm in grams]
    pw = [x.astype(BF16) for x in a_ab]
    pw = [_dot(x, x).astype(BF16) for x in pw]
    tinv = [eye + x for x in a_ab]
    for _ in range((ch - 1).bit_length() - 2):
        both = [_dot(pw[pr], jnp.concatenate([pw[pr], tinv[pr].astype(BF16)], axis=1)) for pr in pairs]
        tinv = [tinv[pr] + both[pr][:, 2 * ch:] for pr in pairs]
        pw = [both[pr][:, :2 * ch].astype(BF16) for pr in pairs]
    tinv = [tinv[pr] + _dot(pw[pr], tinv[pr].astype(BF16)) for pr in pairs]

    heads = range(HEAD_PAIRS)
    st = [state_ref[pr] for pr in heads]
    o_all = []
    for cc in range(n_chunks):
        it = [cc * HEAD_PAIRS + pr for pr in heads]
        st_b = [(st[pr] * e_ref[cc][:, sls[pr]]).astype(BF16) for pr in heads]
        w = [_dot_nt(abar[it[pr]], st_b[pr]) + _dot(a_ak[it[pr]], vs[it[pr]]) for pr in heads]
        u = [_dot(tinv[it[pr]].astype(BF16), w[pr].astype(BF16)) for pr in heads]
        uv = [jnp.concatenate([u[pr].astype(BF16), vs[it[pr]]], axis=0) for pr in heads]
        o_all.append([_dot_nt(rbar[it[pr]], st_b[pr]) + _dot(m_r[it[pr]], uv[pr]) for pr in heads])
        st = [st[pr] * g_tot[cc][:, sls[pr]] + _dot_tn(uv[pr], upd[it[pr]]) for pr in heads]
    for pr in heads:
        state_ref[pr] = st[pr]

    for cc, rows in enumerate(chunk_rows):
        for pr, sl in enumerate(sls):
            o_s = o_all[cc][pr]
            mean = jnp.sum(o_s, axis=-1, keepdims=True) * (1.0 / RWKV_HEAD)
            cen = jnp.where(blockmask, o_s - mean, 0.0)
            var = jnp.sum(cen * cen, axis=-1, keepdims=True) * (1.0 / RWKV_HEAD)
            gn_s = cen * lax.rsqrt(var + RWKV_GN_EPS)
            gn = gn_s[:ch] + gn_s[ch:]
            v = v_ref[rows, sl].astype(F32)
            bonus = head_sums(r_ref[rows, sl].astype(F32) * kmod_of(rows, sl) * prm_ref[2:3, sl]) * v
            z = (gn * prm_ref[3:4, sl] + prm_ref[4:5, sl] + bonus) * g_ref[rows, sl].astype(F32)
            o_ref[rows, sl] = z.astype(o_ref.dtype)


LORA_COLS = DECAY_LORA + AAA_LORA + GATE_LORA
CARRY_COLS = 3 * RWKV_DIM + LORA_COLS


def _rwkv_kernel(has_vres, *refs):
    if has_vres:
        (r_in, k_in, v_in, l_in, mu_ref, mul_ref, w0_ref, a0_ref, w2_ref, a2_ref, g2_ref,
         vf_ref, v0_ref, v1_ref, v2_ref, prm_ref,
         o_ref, vout_ref, state_ref, carry_ref, r_s, k_s, v_s, a_s, g_s, lw_s) = refs
    else:
        (r_in, k_in, v_in, l_in, mu_ref, mul_ref, w0_ref, a0_ref, w2_ref, a2_ref, g2_ref, prm_ref,
         o_ref, vout_ref, state_ref, carry_ref, r_s, k_s, v_s, a_s, g_s, lw_s) = refs
    rows = r_in.shape[0]
    dr = RWKV_DIM

    @pl.when(pl.program_id(1) == 0)
    def _():
        carry_ref[...] = jnp.zeros_like(carry_ref)

    row0 = lax.broadcasted_iota(jnp.int32, (rows, 1), 0) == 0

    def shift_mix(cur_ref, col, mu):
        cur = cur_ref[...].astype(F32)
        last = carry_ref[0:1, col:col + cur.shape[1]]
        sh = jnp.where(row0, last, pltpu.roll(cur, 1, 0))
        return cur + (sh - cur) * mu, cur[rows - 1:rows, :]

    r, r_last = shift_mix(r_in, 0, mu_ref[0:1, :])
    k, k_last = shift_mix(k_in, dr, mu_ref[1:2, :])
    v, v_last = shift_mix(v_in, 2 * dr, mu_ref[2:3, :])
    xl, l_last = shift_mix(l_in, 3 * dr, mul_ref[...])
    carry_ref[0:1, 0:dr] = r_last
    carry_ref[0:1, dr:2 * dr] = k_last
    carry_ref[0:1, 2 * dr:3 * dr] = v_last
    carry_ref[0:1, 3 * dr:] = l_last

    wa = xl[:, :LANES]
    zw = w0_ref[...] + _dot(jnp.tanh(wa).astype(BF16), w2_ref[...])
    lw_s[...] = -jnp.exp(-0.5) * _sigmoid(zw)
    a_s[...] = _sigmoid(a0_ref[...] + _dot(wa.astype(BF16), a2_ref[...]))
    g_s[...] = _dot(_sigmoid(xl[:, LANES:]).astype(BF16), g2_ref[...])
    if has_vres:
        lo_rank = _dot(v.astype(BF16), v1_ref[...])
        mix = _sigmoid(v0_ref[...] + _dot(lo_rank.astype(BF16), v2_ref[...]))
        v = v + (vf_ref[...].astype(F32) - v) * mix
    r_s[...] = r
    k_s[...] = k
    v_s[...] = v
    vout_ref[...] = v.astype(vout_ref.dtype)
    _scan_kernel(r_s, k_s, v_s, a_s, g_s, lw_s, prm_ref, o_ref, state_ref)


def _rwkv_mix(p, prm, scan_prm, v_first, batch, seq):
    rows = SCAN_CHUNK * SCAN_CHUNKS_PER_STEP
    nc = seq // rows
    dr = RWKV_DIM
    has_vres = v_first is not None
    full = lambda a: pl.BlockSpec(a.shape, lambda b, c: (0,) * a.ndim)
    col = lambda cb: (lambda b, c: (b * nc + c, cb))
    in_specs = [pl.BlockSpec((rows, dr), col(COL_R // dr)),
                pl.BlockSpec((rows, dr), col(COL_K // dr)),
                pl.BlockSpec((rows, dr), col(COL_V // dr)),
                pl.BlockSpec((rows, LORA_COLS), col(COL_LORA // LORA_COLS))]
    args = [p] * 4
    small = [prm["mu_rkv"], prm["mu_lora"], prm["w0"], prm["a0"], prm["w2"], prm["a2"], prm["g2"]]
    if has_vres:
        small_v = [prm["v0"], prm["v1"], prm["v2"]]
        in_specs += [full(a) for a in small] + [pl.BlockSpec((rows, dr), col(0))] + [full(a) for a in small_v]
        args += small + [v_first] + small_v
    else:
        in_specs += [full(a) for a in small]
        args += small
    in_specs.append(full(scan_prm))
    args.append(scan_prm)
    blk = pl.BlockSpec((rows, dr), col(0))
    out = jax.ShapeDtypeStruct((batch * seq, dr), BF16)
    return pl.pallas_call(
        functools.partial(_rwkv_kernel, has_vres),
        grid=(batch, nc),
        in_specs=in_specs,
        out_specs=[blk, blk],
        out_shape=[out, out],
        scratch_shapes=[pltpu.VMEM((HEAD_PAIRS, 2 * RWKV_HEAD, LANES), F32),
                        pltpu.VMEM((SUBLANES, CARRY_COLS), F32)] + [pltpu.VMEM((rows, dr), F32)] * 6,
        compiler_params=_cparams(2),
        name="rwkv_mix",
    )(*args)


def _merge_kernel(alpha, z_ref, ya_ref, ga_ref, gb_ref, x_ref, wr_ref, wo_ref, lg_ref, lb_ref, o_ref, ot_ref):
    y_r = _dot(z_ref[...], wr_ref[...])
    mix = _sigmoid(ga_ref[...].astype(F32)) * ya_ref[...].astype(F32) + _sigmoid(gb_ref[...].astype(F32)) * y_r
    mixed = _dot(mix.astype(BF16), wo_ref[...])
    out = _layer_norm(alpha * x_ref[...] + mixed, lg_ref[...], lb_ref[...])
    o_ref[...] = out
    _store_token_rows(ot_ref, out)


def _merge(z, y_attn, p, x, w_r, w_o, ln_g, ln_b, alpha, tm):
    t, d = x.shape
    row = lambda i: (i, 0)
    full = lambda a: pl.BlockSpec(a.shape, lambda i: (0,) * a.ndim)
    return pl.pallas_call(
        functools.partial(_merge_kernel, alpha),
        grid=(t // tm,),
        in_specs=[pl.BlockSpec((tm, d), row), pl.BlockSpec((tm, d), row),
                  pl.BlockSpec((tm, d), lambda i: (i, COL_GA // d)),
                  pl.BlockSpec((tm, d), lambda i: (i, COL_GB // d)),
                  pl.BlockSpec((tm, d), row), full(w_r), full(w_o), full(ln_g), full(ln_b)],
        out_specs=[pl.BlockSpec((tm, d), row), pl.BlockSpec((tm * SUBLANES, LANES), row)],
        out_shape=[jax.ShapeDtypeStruct((t, d), F32), jax.ShapeDtypeStruct((t * SUBLANES, LANES), F32)],
        compiler_params=_cparams(1),
        name="merge_ln1",
    )(z, y_attn, p, p, x, w_r, w_o, ln_g, ln_b)


def _split3(x):
    hi = x.astype(BF16)
    rem = x - hi.astype(F32)
    mid = rem.astype(BF16)
    low = (rem - mid.astype(F32)).astype(BF16)
    return hi, mid, low


def _router_kernel(x_ref, w_ref, b_ref, idx_ref, gate_ref, rank_ref, cnt_ref, carry_ref):
    @pl.when(pl.program_id(0) == 0)
    def _():
        carry_ref[...] = jnp.zeros_like(carry_ref)

    x0, x1, x2 = _split3(x_ref[...])
    w0, w1, w2 = w_ref[0], w_ref[1], w_ref[2]
    logits = (_dot(x0, w0) + (_dot(x0, w1) + _dot(x1, w0)) +
              (_dot(x0, w2) + _dot(x1, w1) + _dot(x2, w0))) + b_ref[...]
    tm = logits.shape[0]
    lane = lax.broadcasted_iota(jnp.int32, (tm, LANES), 1).astype(F32)
    work = logits
    idx_out = jnp.zeros((tm, LANES), F32)
    val_out = jnp.zeros((tm, LANES), F32)
    top = None
    den = jnp.zeros((tm, 1), F32)
    hits = []
    for kk in range(TOP_K):
        m = jnp.max(work, axis=-1, keepdims=True)
        sel = jnp.min(jnp.where(work == m, lane, float(LANES)), axis=-1, keepdims=True)
        if kk == 0:
            top = m
        e = jnp.exp(m - top)
        den = den + e
        idx_out = jnp.where(lane == float(kk), sel, idx_out)
        val_out = jnp.where(lane == float(kk), e, val_out)
        hits.append(lane == sel)
        work = jnp.where(hits[-1], -jnp.inf, work)
    idx_ref[...] = idx_out.astype(jnp.int32)
    gate_ref[...] = val_out / den

    chosen = jnp.where(hits[0] | hits[1] | hits[2] | hits[3], 1.0, 0.0)
    r_i = lax.broadcasted_iota(jnp.int32, (tm, tm), 0)
    c_i = lax.broadcasted_iota(jnp.int32, (tm, tm), 1)
    earlier = jnp.where(r_i > c_i, 1.0, 0.0).astype(BF16)
    before = carry_ref[0:1, :] + _dot(earlier, chosen.astype(BF16))
    rank_out = jnp.zeros((tm, LANES), F32)
    for kk in range(TOP_K):
        rk = jnp.sum(jnp.where(hits[kk], before, 0.0), axis=-1, keepdims=True)
        rank_out = jnp.where(lane == float(kk), rk, rank_out)
    rank_ref[...] = rank_out.astype(jnp.int32)
    carry_ref[0:1, :] = carry_ref[0:1, :] + jnp.sum(chosen, axis=0, keepdims=True)
    cnt_ref[...] = carry_ref[...].astype(jnp.int32)


def _router(x, w3, b, tm):
    t, d = x.shape
    row = lambda i: (i, 0)
    return pl.pallas_call(
        _router_kernel,
        grid=(t // tm,),
        in_specs=[pl.BlockSpec((tm, d), row), pl.BlockSpec(w3.shape, lambda i: (0, 0, 0)),
                  pl.BlockSpec(b.shape, lambda i: (0, 0))],
        out_specs=[pl.BlockSpec((tm, LANES), row)] * 3 + [pl.BlockSpec((8, LANES), lambda i: (0, 0))],
        out_shape=[jax.ShapeDtypeStruct((t, LANES), jnp.int32), jax.ShapeDtypeStruct((t, LANES), F32),
                   jax.ShapeDtypeStruct((t, LANES), jnp.int32), jax.ShapeDtypeStruct((8, LANES), jnp.int32)],
        scratch_shapes=[pltpu.VMEM((8, LANES), F32)],
        compiler_params=_cparams(1),
        name="router",
    )(x, w3, b)


SUBLANES = 8
TOKEN_TILES = D_MODEL // LANES


def _row_copy(src_hbm, dst_vmem, sem, src_row8, dst_tok):
    if not isinstance(src_row8, int):
        src_row8 = pl.multiple_of(src_row8, SUBLANES)
    return pltpu.make_async_copy(src_hbm.at[pl.ds(src_row8, SUBLANES), :],
                                 dst_vmem.at[pl.ds(dst_tok * SUBLANES, SUBLANES), :], sem)


def _load_token_rows(ref, n):
    return jnp.concatenate([ref[pl.ds(s, n, stride=SUBLANES), :] for s in range(TOKEN_TILES)], axis=1)


def _store_token_rows(ref, val):
    n = val.shape[0]
    for s in range(TOKEN_TILES):
        ref[pl.ds(s, n, stride=SUBLANES), :] = val[:, s * LANES:(s + 1) * LANES]


DEINT_GROUP = 2 * LANES


def _deint_kernel(w_ref, sel_ref, g_ref, l_ref):
    for c in range(w_ref.shape[1] // DEINT_GROUP):
        blk = w_ref[:, c * DEINT_GROUP:(c + 1) * DEINT_GROUP].astype(BF16)
        both = _dot(blk, sel_ref[...])
        g_ref[:, c * LANES:(c + 1) * LANES] = both[:, :LANES].astype(BF16)
        l_ref[:, c * LANES:(c + 1) * LANES] = both[:, LANES:].astype(BF16)


def _deinterleave(w, tm):
    rows, cols = w.shape
    src = jnp.arange(DEINT_GROUP)[:, None]
    dst = jnp.arange(DEINT_GROUP)[None, :]
    sel = jnp.where(dst < LANES, src == 2 * dst, src == 2 * (dst - LANES) + 1).astype(BF16)
    half = jax.ShapeDtypeStruct((rows, cols // 2), BF16)
    return pl.pallas_call(
        _deint_kernel,
        grid=(rows // tm,),
        in_specs=[pl.BlockSpec((tm, cols), lambda i: (i, 0)), pl.BlockSpec(sel.shape, lambda i: (0, 0))],
        out_specs=[pl.BlockSpec((tm, cols // 2), lambda i: (i, 0))] * 2,
        out_shape=[half, half],
        compiler_params=_cparams(1),
        name="deinterleave",
    )(w, sel)


def _expert_kernel(be_ref, nu_ref, idx0_ref, idxn_ref, x_hbm, w1g_ref, w1l_ref, b1g_ref, b1l_ref, w2_ref, b2_ref,
                   o_ref, xbuf, sems):
    i = pl.program_id(0)
    last = pl.num_programs(0) - 1
    slot = lax.rem(i, 2)
    used = i < nu_ref[0]

    def issue(idx_ref, s, start=0, stop=MOE_BLOCK):
        for r in range(start, stop):
            _row_copy(x_hbm, xbuf.at[s], sems.at[s], idx_ref[0, 0, r], r).start()

    def drain(s):
        for r in range(MOE_BLOCK):
            _row_copy(x_hbm, xbuf.at[s], sems.at[s], 0, r).wait()

    @pl.when(i == 0)
    def _():
        issue(idx0_ref, 0)

    @pl.when(jnp.logical_not(used))
    def _():
        issue(idxn_ref, 1 - slot)
        drain(slot)
        o_ref[...] = jnp.zeros_like(o_ref)

    @pl.when(used)
    def _():
        q = MOE_BLOCK // 4
        issue(idxn_ref, 1 - slot, 0, q)
        drain(slot)
        xb = _load_token_rows(xbuf.at[slot], MOE_BLOCK).astype(BF16)
        h_glu = _dot(xb, w1g_ref[...]) + b1g_ref[...]
        issue(idxn_ref, 1 - slot, q, 2 * q)
        h_lin = _dot(xb, w1l_ref[...]) + b1l_ref[...]
        issue(idxn_ref, 1 - slot, 2 * q, 3 * q)
        x_glu = jnp.minimum(h_glu, SWIGLU_LIMIT)
        x_lin = jnp.clip(h_lin, -SWIGLU_LIMIT, SWIGLU_LIMIT)
        act = x_glu * _sigmoid(SWIGLU_ALPHA * x_glu) * (x_lin + 1.0)
        issue(idxn_ref, 1 - slot, 3 * q, MOE_BLOCK)
        _store_token_rows(o_ref, _dot(act.astype(BF16), w2_ref[...]) + b2_ref[...])

    @pl.when(i == last)
    def _():
        drain(1 - slot)


def _experts(x_tiles, row_tok, block_exp, n_used, w1g, w1l, b1g, b1l, w2, b2):
    d = D_MODEL
    nblk = row_tok.shape[0] // MOE_BLOCK
    de = w1g.shape[2]
    idx = (row_tok * SUBLANES).reshape(nblk, 1, MOE_BLOCK)
    blk_rows = MOE_BLOCK * SUBLANES
    wmap = lambda i, be, nu: (be[i], 0, 0)
    grid_spec = pltpu.PrefetchScalarGridSpec(
        num_scalar_prefetch=2,
        grid=(nblk,),
        in_specs=[pl.BlockSpec((1, 1, MOE_BLOCK), lambda i, be, nu: (0, 0, 0), memory_space=pltpu.SMEM),
                  pl.BlockSpec((1, 1, MOE_BLOCK), lambda i, be, nu: (jnp.minimum(i + 1, nblk - 1), 0, 0),
                               memory_space=pltpu.SMEM),
                  pl.BlockSpec(memory_space=pl.ANY),
                  pl.BlockSpec((None, d, de), wmap), pl.BlockSpec((None, d, de), wmap),
                  pl.BlockSpec((None, 1, de), wmap), pl.BlockSpec((None, 1, de), wmap),
                  pl.BlockSpec((None, de, d), wmap), pl.BlockSpec((None, 1, d), wmap)],
        out_specs=pl.BlockSpec((blk_rows, LANES), lambda i, be, nu: (i, 0)),
        scratch_shapes=[pltpu.VMEM((2, blk_rows, LANES), F32), pltpu.SemaphoreType.DMA((2,))],
    )
    return pl.pallas_call(
        _expert_kernel,
        grid_spec=grid_spec,
        out_shape=jax.ShapeDtypeStruct((nblk * blk_rows, LANES), F32),
        compiler_params=_cparams(1),
        name="moe_experts",
    )(block_exp, n_used, idx, idx, x_tiles, w1g, w1l, b1g, b1l, w2, b2)


def _combine_kernel(alpha, idx0_ref, idxn_ref, y_hbm, gate_ref, x_ref, lg_ref, lb_ref, o_ref, buf, sems):
    tc = x_ref.shape[0]
    i = pl.program_id(0)
    last = pl.num_programs(0) - 1
    slot = lax.rem(i, 2)

    def issue(idx_ref, s):
        for r in range(tc):
            for kk in range(TOP_K):
                _row_copy(y_hbm, buf.at[s, kk], sems.at[s], idx_ref[0, 0, r * TOP_K + kk], r).start(priority=kk % 2)

    def drain(s):
        for r in range(tc * TOP_K):
            _row_copy(y_hbm, buf.at[s, 0], sems.at[s], 0, 0).wait()

    @pl.when(i == 0)
    def _():
        issue(idx0_ref, 0)

    issue(idxn_ref, 1 - slot)
    drain(slot)

    @pl.when(i == last)
    def _():
        drain(1 - slot)

    gates = gate_ref[...]
    y = _load_token_rows(buf.at[slot, 0], tc) * gates[:, 0:1]
    for kk in range(1, TOP_K):
        y = y + _load_token_rows(buf.at[slot, kk], tc) * gates[:, kk:kk + 1]
    o_ref[...] = _layer_norm(alpha * x_ref[...] + y, lg_ref[...], lb_ref[...])


def _combine(y_rows, dest, gates, x, ln_g, ln_b, alpha, tc):
    t, d = x.shape
    steps = t // tc
    row = lambda i: (i, 0)
    full = lambda a: pl.BlockSpec(a.shape, lambda i: (0,) * a.ndim)
    idx = (dest * SUBLANES).reshape(steps, 1, tc * TOP_K)
    return pl.pallas_call(
        functools.partial(_combine_kernel, alpha),
        grid=(steps,),
        in_specs=[pl.BlockSpec((1, 1, tc * TOP_K), lambda i: (0, 0, 0), memory_space=pltpu.SMEM),
                  pl.BlockSpec((1, 1, tc * TOP_K), lambda i: (jnp.minimum(i + 1, steps - 1), 0, 0),
                               memory_space=pltpu.SMEM),
                  pl.BlockSpec(memory_space=pl.ANY),
                  pl.BlockSpec((tc, LANES), row), pl.BlockSpec((tc, d), row), full(ln_g), full(ln_b)],
        out_specs=pl.BlockSpec((tc, d), row),
        out_shape=jax.ShapeDtypeStruct((t, d), F32),
        scratch_shapes=[pltpu.VMEM((2, TOP_K, tc * SUBLANES, LANES), F32), pltpu.SemaphoreType.DMA((2,))],
        compiler_params=_cparams(1),
        name="moe_combine_ln2",
    )(idx, idx, y_rows, gates, x, ln_g, ln_b)


def _routing_tables(top_idx, rank, counts):
    t = top_idx.shape[0]
    n_assign = t * TOP_K
    e_flat = top_idx.reshape(n_assign)
    padded = ((counts + MOE_BLOCK - 1) // MOE_BLOCK) * MOE_BLOCK
    pend = jnp.cumsum(padded)
    pstart = pend - padded
    dest = (pstart[e_flat] + rank.reshape(n_assign)).astype(jnp.int32)
    n_blocks = -(-n_assign // MOE_BLOCK) + N_EXPERTS
    tok_flat = jnp.arange(n_assign, dtype=jnp.int32) // TOP_K
    row_tok = jnp.zeros((n_blocks * MOE_BLOCK,), jnp.int32).at[dest].set(
        tok_flat, unique_indices=True, mode="promise_in_bounds")
    block_start = jnp.arange(n_blocks, dtype=jnp.int32) * MOE_BLOCK
    block_exp = jnp.minimum(jnp.sum(pend[None, :] <= block_start[:, None], axis=1), N_EXPERTS - 1).astype(jnp.int32)
    n_used = (pend[-1] // MOE_BLOCK).astype(jnp.int32).reshape(1)
    return dest, row_tok, block_exp, n_used


def _pad_rows(w, rows, at):
    out = jnp.zeros((rows, w.shape[1]), w.dtype)
    return out.at[at:at + w.shape[0]].set(w)


def _tile(n, pref):
    return pref if n % pref == 0 else n


def kernel(x, w_in, attn_sinks, w_attn_o, rwkv_mu, rwkv_w0, rwkv_w2, rwkv_a0, rwkv_a2, rwkv_g2, rwkv_k_k, rwkv_k_a, rwkv_r_k, rwkv_v0, rwkv_v1, rwkv_v2, rwkv_lnx_w, rwkv_lnx_b, w_rwkv_o, w_out, ln1_g, ln1_b, router_w, router_b, expert_w1, expert_b1, expert_w2, expert_b2, ln2_g, ln2_b):
    batch, seq, d = x.shape
    depth = w_in.shape[0]
    t = batch * seq
    alpha = (2.0 * depth) ** 0.25
    dr = RWKV_DIM
    tables = _rope_tables(seq)
    xf = x.reshape(t, d)
    v_first = None
    o_att, o_rw = ATT_Q + 2 * ATT_KV, ATT_Q + 2 * ATT_KV + 3 * dr + DECAY_LORA + AAA_LORA + GATE_LORA

    for l in range(depth):
        wl = w_in[l]
        wp = jnp.concatenate([
            wl[:, :ATT_Q], wl[:, o_att:o_att + 3 * dr], wl[:, o_rw:o_rw + 2 * d],
            wl[:, ATT_Q:o_att], wl[:, o_att + 3 * dr:o_rw]], axis=1).astype(BF16)
        p = _project(xf, wp, _tile(t, 512), P_COLS // 3)

        y_attn = _attention(p, attn_sinks[l], tables, w_attn_o[l].astype(BF16), batch, seq)

        mu = rwkv_mu[l]
        prm = {
            "mu_rkv": mu[:3 * dr].reshape(3, dr),
            "mu_lora": mu[3 * dr:].reshape(1, -1),
            "w0": rwkv_w0[l].reshape(1, dr), "a0": rwkv_a0[l].reshape(1, dr),
            "w2": _pad_rows(rwkv_w2[l], LANES, 0).astype(BF16),
            "a2": _pad_rows(rwkv_a2[l], LANES, DECAY_LORA).astype(BF16),
            "g2": rwkv_g2[l].astype(BF16),
        }
        if l > 0:
            prm["v0"] = rwkv_v0[l - 1].reshape(1, dr)
            prm["v1"] = jnp.zeros((dr, LANES), F32).at[:, :MV_LORA].set(rwkv_v1[l - 1]).astype(BF16)
            prm["v2"] = _pad_rows(rwkv_v2[l - 1], LANES, 0).astype(BF16)
        scan_prm = jnp.stack([rwkv_k_k[l], rwkv_k_a[l], rwkv_r_k[l].reshape(dr), rwkv_lnx_w[l], rwkv_lnx_b[l],
                              jnp.zeros((dr,), F32), jnp.zeros((dr,), F32), jnp.zeros((dr,), F32)])
        z, v_ = _rwkv_mix(p, prm, scan_prm, v_first, batch, seq)
        if l == 0:
            v_first = v_

        x1, x1_tiles = _merge(z, y_attn, p, xf, w_rwkv_o[l].astype(BF16), w_out[l].astype(BF16),
                    ln1_g[l].reshape(1, d), ln1_b[l].reshape(1, d), alpha, _tile(t, 512))

        rw = jnp.zeros((d, LANES), F32).at[:, :N_EXPERTS].set(router_w[l])
        rb = jnp.full((1, LANES), -1e30, F32).at[0, :N_EXPERTS].set(router_b[l])
        top_idx, gates, rank, counts = _router(x1, jnp.stack(_split3(rw)), rb, _tile(t, 512))
        dest, row_tok, block_exp, n_used = _routing_tables(top_idx[:, :TOP_K], rank[:, :TOP_K],
                                                           counts[0, :N_EXPERTS])

        b1 = expert_b1[l]
        w1g, w1l = _deinterleave(expert_w1[l].reshape(N_EXPERTS * d, 2 * D_EXPERT), 512)
        y_rows = _experts(x1_tiles, row_tok, block_exp, n_used,
                          w1g.reshape(N_EXPERTS, d, D_EXPERT), w1l.reshape(N_EXPERTS, d, D_EXPERT),
                          b1[:, None, 0::2], b1[:, None, 1::2],
                          expert_w2[l].astype(BF16), expert_b2[l][:, None, :])
        xf = _combine(y_rows, dest, gates, x1, ln2_g[l].reshape(1, d), ln2_b[l].reshape(1, d),
                      alpha, _tile(t, 128))
    return xf.reshape(batch, seq, d)
```

```python
import functools

import jax
import jax.numpy as jnp
from jax import lax
from jax.experimental import pallas as pl
from jax.experimental.pallas import tpu as pltpu

F32 = jnp.float32
BF16 = jnp.bfloat16

D_MODEL = 1024
HEAD_DIM = 64
N_Q_HEADS = 16
N_KV_HEADS = 4
WINDOW = 128
ATTN_BLOCK = 128
ROPE_THETA = 10000.0
ATT_Q = N_Q_HEADS * HEAD_DIM
ATT_KV = N_KV_HEADS * HEAD_DIM

RWKV_HEAD = 64
RWKV_HEADS = 16
RWKV_DIM = RWKV_HEADS * RWKV_HEAD
DECAY_LORA = 64
AAA_LORA = 64
MV_LORA = 32
GATE_LORA = 128
RWKV_GN_EPS = 64e-5

N_EXPERTS = 32
TOP_K = 4
D_EXPERT = 1024
SWIGLU_LIMIT = 7.0
SWIGLU_ALPHA = 1.702
MOE_BLOCK = 256

LN_EPS = 1e-5

LANES = 128
HEAD_PAIRS = RWKV_HEADS // 2
SCAN_CHUNK = 64
SCAN_CHUNKS_PER_STEP = 2
VMEM_LIMIT = 56 * 1024 * 1024

COL_Q, COL_R, COL_K, COL_V, COL_GA, COL_GB = 0, 1024, 2048, 3072, 4096, 5120
COL_AK, COL_AV, COL_LORA = 6144, 6400, 6656
P_COLS = 6912


def _cparams(n_axes):
    return pltpu.CompilerParams(dimension_semantics=("arbitrary",) * n_axes,
                                vmem_limit_bytes=VMEM_LIMIT)


def _dot(a, b):
    return jnp.dot(a, b, preferred_element_type=F32)


def _dot_nt(a, b):
    return lax.dot_general(a, b, (((1,), (1,)), ((), ())), preferred_element_type=F32)


def _dot_tn(a, b):
    return lax.dot_general(a, b, (((0,), (0,)), ((), ())), preferred_element_type=F32)


def _sigmoid(z):
    return 1.0 / (1.0 + jnp.exp(-z))


def _layer_norm(h, g, b):
    mu = jnp.mean(h, axis=-1, keepdims=True)
    c = h - mu
    var = jnp.mean(c * c, axis=-1, keepdims=True)
    return c * lax.rsqrt(var + LN_EPS) * g + b


def _proj_kernel(x_ref, w_ref, o_ref):
    o_ref[...] = _dot(x_ref[...].astype(BF16), w_ref[...]).astype(o_ref.dtype)


def _project(x, w, tm, tn):
    m, k = x.shape
    n = w.shape[1]
    return pl.pallas_call(
        _proj_kernel,
        grid=(n // tn, m // tm),
        in_specs=[pl.BlockSpec((tm, k), lambda j, i: (i, 0)),
                  pl.BlockSpec((k, tn), lambda j, i: (0, j))],
        out_specs=pl.BlockSpec((tm, tn), lambda j, i: (i, j)),
        out_shape=jax.ShapeDtypeStruct((m, n), BF16),
        compiler_params=_cparams(2),
        name="proj",
    )(x, w)


def _rope(x, c, s1, s2):
    return x * c + pltpu.roll(x, LANES - HEAD_DIM // 2, 1) * s1 + pltpu.roll(x, HEAD_DIM // 2, 1) * s2


def _attn_kernel(sink_ref, q_ref, kc_ref, kp_ref, vc_ref, vp_ref, tc_ref, tp_ref, wo_ref, o_ref):
    n = pl.program_id(1)
    blk = ATTN_BLOCK
    lane = lax.broadcasted_iota(jnp.int32, (1, LANES), 1)
    lo = lane < HEAD_DIM
    cq, s1q, s2q = tc_ref[0], tc_ref[1], tc_ref[2]
    cp, s1p, s2p = tp_ref[0], tp_ref[1], tp_ref[2]

    row = lax.broadcasted_iota(jnp.int32, (blk, 2 * blk), 0)
    col = lax.broadcasted_iota(jnp.int32, (blk, 2 * blk), 1)
    valid = (col > row) & (col <= row + WINDOW) & ((n > 0) | (col >= blk))

    def lo_hi(t, off):
        sw = pltpu.roll(t, HEAD_DIM, 1)
        t_lo = jnp.where(lo, t if off == 0 else sw, 0.0).astype(BF16)
        t_hi = jnp.where(lo, 0.0, sw if off == 0 else t).astype(BF16)
        return t_lo, t_hi

    kv_tiles = []
    for c in range(ATT_KV // LANES):
        sl = slice(c * LANES, (c + 1) * LANES)
        kc = _rope(kc_ref[:, sl].astype(F32), cq, s1q, s2q)
        kp = _rope(kp_ref[:, sl].astype(F32), cp, s1p, s2p)
        kt = jnp.concatenate([kp, kc], axis=0)
        vt = jnp.concatenate([vp_ref[:, sl].astype(F32), vc_ref[:, sl].astype(F32)], axis=0)
        kv_tiles.append((kt, vt))

    k_m, v_st = [], []
    for g in range(N_KV_HEADS):
        kt, vt = kv_tiles[g // 2]
        k_m.append(lo_hi(kt, g % 2))
        v_st.append(jnp.concatenate(lo_hi(vt, g % 2), axis=0))
    n_pairs = N_Q_HEADS // 2
    log2e = 1.4426950408889634
    qp = [(_rope(q_ref[:, j * LANES:(j + 1) * LANES].astype(F32), cq, s1q, s2q)
           * (HEAD_DIM ** -0.5 * log2e)).astype(BF16) for j in range(n_pairs)]
    scores = [_dot_nt(qp[hh // 2], k_m[hh // 4][hh % 2]) for hh in range(N_Q_HEADS)]
    probs, rden = [], []
    for hh in range(N_Q_HEADS):
        s = jnp.where(valid, scores[hh], -1e30)
        sink = sink_ref[hh] * log2e
        m = jnp.maximum(jnp.max(s, axis=-1, keepdims=True), sink)
        p = jnp.exp2(s - m)
        rden.append(1.0 / (jnp.sum(p, axis=-1, keepdims=True) + jnp.exp2(sink - m)))
        probs.append(p.astype(BF16))
    outs = [_dot(jnp.concatenate([probs[2 * j], probs[2 * j + 1]], axis=1), v_st[j // 2])
            * jnp.where(lo, rden[2 * j], rden[2 * j + 1]) for j in range(n_pairs)]
    att = jnp.concatenate(outs, axis=1).astype(BF16)
    o_ref[...] = _dot(att, wo_ref[...]).astype(o_ref.dtype)


def _attention(p, sinks, tables, w_o, batch, seq):
    nb = seq // ATTN_BLOCK
    blk = ATTN_BLOCK
    kcol, vcol = COL_AK // ATT_KV, COL_AV // ATT_KV
    cur = lambda b, n: b * nb + n
    prev = lambda b, n: b * nb + jnp.maximum(n - 1, 0)
    return pl.pallas_call(
        _attn_kernel,
        grid=(batch, nb),
        in_specs=[
            pl.BlockSpec(memory_space=pltpu.SMEM),
            pl.BlockSpec((blk, ATT_Q), lambda b, n: (cur(b, n), COL_Q // ATT_Q)),
            pl.BlockSpec((blk, ATT_KV), lambda b, n: (cur(b, n), kcol)),
            pl.BlockSpec((blk, ATT_KV), lambda b, n: (prev(b, n), kcol)),
            pl.BlockSpec((blk, ATT_KV), lambda b, n: (cur(b, n), vcol)),
            pl.BlockSpec((blk, ATT_KV), lambda b, n: (prev(b, n), vcol)),
            pl.BlockSpec((3, blk, LANES), lambda b, n: (0, n, 0)),
            pl.BlockSpec((3, blk, LANES), lambda b, n: (0, jnp.maximum(n - 1, 0), 0)),
            pl.BlockSpec((ATT_Q, D_MODEL), lambda b, n: (0, 0)),
        ],
        out_specs=pl.BlockSpec((blk, D_MODEL), lambda b, n: (cur(b, n), 0)),
        out_shape=jax.ShapeDtypeStruct((batch * seq, D_MODEL), BF16),
        compiler_params=_cparams(2),
        name="attn",
    )(sinks, p, p, p, p, p, tables, tables, w_o)


def _rope_tables(seq):
    pos = jnp.arange(seq, dtype=F32)
    inv = ROPE_THETA ** (-jnp.arange(0, HEAD_DIM, 2, dtype=F32) / HEAD_DIM)
    ang = pos[:, None] * inv[None, :]
    cos, sin = jnp.cos(ang), jnp.sin(ang)
    zero = jnp.zeros_like(sin)
    c = jnp.tile(cos, (1, LANES // (HEAD_DIM // 2)))
    s1 = jnp.tile(jnp.concatenate([-sin, zero], axis=1), (1, LANES // HEAD_DIM))
    s2 = jnp.tile(jnp.concatenate([zero, sin], axis=1), (1, LANES // HEAD_DIM))
    return jnp.stack([c, s1, s2])


def _scan_kernel(r_ref, k_ref, v_ref, a_ref, g_ref, lw_ref, prm_ref, o_ref, state_ref):
    ch = SCAN_CHUNK
    c_id = pl.program_id(1)

    @pl.when(c_id == 0)
    def _():
        state_ref[...] = jnp.zeros_like(state_ref)

    n_chunks = lw_ref.shape[0] // ch
    chunk_rows = [slice(cc * ch, (cc + 1) * ch) for cc in range(n_chunks)]
    tri = (lax.broadcasted_iota(jnp.int32, (ch, ch), 0) >= lax.broadcasted_iota(jnp.int32, (ch, ch), 1)
           ).astype(BF16)
    e_in, e_ex, e_inv, e_end, g_tot, e_ref = [], [], [], [], [], []
    for rows in chunk_rows:
        lw = lw_ref[rows, :]
        hi = lw.astype(BF16)
        rem = lw - hi.astype(F32)
        mid = rem.astype(BF16)
        low = (rem - mid.astype(F32)).astype(BF16)
        cum = _dot(tri, hi) + _dot(tri, mid) + _dot(tri, low)
        ref_row = cum[ch // 2 - 1:ch // 2, :]
        total = cum[ch - 1:ch, :]
        e_in.append(jnp.exp(cum - ref_row))
        e_ex.append(jnp.exp(cum - lw - ref_row))
        e_inv.append(jnp.exp(ref_row - cum))
        e_end.append(jnp.exp(total - cum))
        g_tot.append(jnp.exp(total))
        e_ref.append(jnp.exp(ref_row))

    lane = lax.broadcasted_iota(jnp.int32, (1, LANES), 1)
    lo = lane < RWKV_HEAD
    r2 = lax.broadcasted_iota(jnp.int32, (2 * ch, 2 * ch), 0)
    c2 = lax.broadcasted_iota(jnp.int32, (2 * ch, 2 * ch), 1)
    same = ((r2 < ch) & (c2 < ch)) | ((r2 >= ch) & (c2 >= ch))
    tri_strict = same & (r2 > c2)
    tri_incl = same & (r2 >= c2)
    eye = jnp.where(r2 == c2, 1.0, 0.0)
    rb = lax.broadcasted_iota(jnp.int32, (2 * ch, LANES), 0)
    cb = lax.broadcasted_iota(jnp.int32, (2 * ch, LANES), 1)
    blockmask = ((rb < ch) & (cb < RWKV_HEAD)) | ((rb >= ch) & (cb >= RWKV_HEAD))

    def stack(x):
        return jnp.concatenate([jnp.where(lo, x, 0.0), jnp.where(lo, 0.0, x)], axis=0)

    def head_sums(x):
        s_e = jnp.sum(jnp.where(lo, x, 0.0), axis=-1, keepdims=True)
        s_o = jnp.sum(jnp.where(lo, 0.0, x), axis=-1, keepdims=True)
        return jnp.where(lo, s_e, s_o)

    sls = [slice(pr * LANES, (pr + 1) * LANES) for pr in range(HEAD_PAIRS)]
    items = [(cc, pr) for cc in range(n_chunks) for pr in range(HEAD_PAIRS)]
    pairs = range(len(items))

    def kmod_of(rows, sl):
        return k_ref[rows, sl].astype(F32) * (1.0 + (a_ref[rows, sl].astype(F32) - 1.0) * prm_ref[1:2, sl])

    abar, rbar, vs, lhs, rhs, upd = [], [], [], [], [], []
    for cc, pr in items:
        rows, sl = chunk_rows[cc], sls[pr]
        a = a_ref[rows, sl].astype(F32)
        kx = k_ref[rows, sl].astype(F32) * prm_ref[0:1, sl]
        kk = kx / jnp.maximum(jnp.sqrt(head_sums(kx * kx)), 1e-12)
        beta = kk * a
        kmod = kmod_of(rows, sl)
        ab = stack(-kk * e_ex[cc][:, sl]).astype(BF16)
        rb_ = stack(r_ref[rows, sl].astype(F32) * e_in[cc][:, sl]).astype(BF16)
        abar.append(ab)
        rbar.append(rb_)
        lhs.append(jnp.concatenate([ab, rb_], axis=0))
        rhs.append(jnp.concatenate([stack(beta * e_inv[cc][:, sl]).astype(BF16),
                                    stack(kmod * e_inv[cc][:, sl]).astype(BF16)], axis=0))
        upd.append(jnp.concatenate([stack(beta * e_end[cc][:, sl]).astype(BF16),
                                    stack(kmod * e_end[cc][:, sl]).astype(BF16)], axis=0))
        vs.append(stack(v_ref[rows, sl].astype(F32)).astype(BF16))

    grams = [_dot_nt(lhs[pr], rhs[pr]) for pr in pairs]
    a_ak = [jnp.where(tri_strict, gm[:2 * ch, 2 * ch:], 0.0).astype(BF16) for gm in grams]
    m_r = [jnp.concatenate([jnp.where(tri_incl, gm[2 * ch:, :2 * ch], 0.0),
                            jnp.where(tri_incl, gm[2 * ch:, 2 * ch:], 0.0)], axis=1).astype(BF16) for gm in grams]

    a_ab = [jnp.where(tri_strict, gm[:2 * ch, :2 * ch], 0.0) for gm in grams]
    pw = [x.astype(BF16) for x in a_ab]
    pw = [_dot(x, x).astype(BF16) for x in pw]
    tinv = [eye + x for x in a_ab]
    for _ in range((ch - 1).bit_length() - 2):
        both = [_dot(pw[pr], jnp.concatenate([pw[pr], tinv[pr].astype(BF16)], axis=1)) for pr in pairs]
        tinv = [tinv[pr] + both[pr][:, 2 * ch:] for pr in pairs]
        pw = [both[pr][:, :2 * ch].astype(BF16) for pr in pairs]
    tinv = [tinv[pr] + _dot(pw[pr], tinv[pr].astype(BF16)) for pr in pairs]

    heads = range(HEAD_PAIRS)
    st = [state_ref[pr] for pr in heads]
    o_all = []
    for cc in range(n_chunks):
        it = [cc * HEAD_PAIRS + pr for pr in heads]
        st_b = [(st[pr] * e_ref[cc][:, sls[pr]]).astype(BF16) for pr in heads]
        w = [_dot_nt(abar[it[pr]], st_b[pr]) + _dot(a_ak[it[pr]], vs[it[pr]]) for pr in heads]
        u = [_dot(tinv[it[pr]].astype(BF16), w[pr].astype(BF16)) for pr in heads]
        uv = [jnp.concatenate([u[pr].astype(BF16), vs[it[pr]]], axis=0) for pr in heads]
        o_all.append([_dot_nt(rbar[it[pr]], st_b[pr]) + _dot(m_r[it[pr]], uv[pr]) for pr in heads])
        st = [st[pr] * g_tot[cc][:, sls[pr]] + _dot_tn(uv[pr], upd[it[pr]]) for pr in heads]
    for pr in heads:
        state_ref[pr] = st[pr]

    for cc, rows in enumerate(chunk_rows):
        for pr, sl in enumerate(sls):
            o_s = o_all[cc][pr]
            mean = jnp.sum(o_s, axis=-1, keepdims=True) * (1.0 / RWKV_HEAD)
            cen = jnp.where(blockmask, o_s - mean, 0.0)
            var = jnp.sum(cen * cen, axis=-1, keepdims=True) * (1.0 / RWKV_HEAD)
            gn_s = cen * lax.rsqrt(var + RWKV_GN_EPS)
            gn = gn_s[:ch] + gn_s[ch:]
            v = v_ref[rows, sl].astype(F32)
            bonus = head_sums(r_ref[rows, sl].astype(F32) * kmod_of(rows, sl) * prm_ref[2:3, sl]) * v
            z = (gn * prm_ref[3:4, sl] + prm_ref[4:5, sl] + bonus) * g_ref[rows, sl].astype(F32)
            o_ref[rows, sl] = z.astype(o_ref.dtype)


LORA_COLS = DECAY_LORA + AAA_LORA + GATE_LORA
CARRY_COLS = 3 * RWKV_DIM + LORA_COLS


def _rwkv_kernel(has_vres, *refs):
    if has_vres:
        (r_in, k_in, v_in, l_in, mu_ref, mul_ref, w0_ref, a0_ref, w2_ref, a2_ref, g2_ref,
         vf_ref, v0_ref, v1_ref, v2_ref, prm_ref,
         o_ref, vout_ref, state_ref, carry_ref, r_s, k_s, v_s, a_s, g_s, lw_s) = refs
    else:
        (r_in, k_in, v_in, l_in, mu_ref, mul_ref, w0_ref, a0_ref, w2_ref, a2_ref, g2_ref, prm_ref,
         o_ref, vout_ref, state_ref, carry_ref, r_s, k_s, v_s, a_s, g_s, lw_s) = refs
    rows = r_in.shape[0]
    dr = RWKV_DIM

    @pl.when(pl.program_id(1) == 0)
    def _():
        carry_ref[...] = jnp.zeros_like(carry_ref)

    row0 = lax.broadcasted_iota(jnp.int32, (rows, 1), 0) == 0

    def shift_mix(cur_ref, col, mu):
        cur = cur_ref[...].astype(F32)
        last = carry_ref[0:1, col:col + cur.shape[1]]
        sh = jnp.where(row0, last, pltpu.roll(cur, 1, 0))
        return cur + (sh - cur) * mu, cur[rows - 1:rows, :]

    r, r_last = shift_mix(r_in, 0, mu_ref[0:1, :])
    k, k_last = shift_mix(k_in, dr, mu_ref[1:2, :])
    v, v_last = shift_mix(v_in, 2 * dr, mu_ref[2:3, :])
    xl, l_last = shift_mix(l_in, 3 * dr, mul_ref[...])
    carry_ref[0:1, 0:dr] = r_last
    carry_ref[0:1, dr:2 * dr] = k_last
    carry_ref[0:1, 2 * dr:3 * dr] = v_last
    carry_ref[0:1, 3 * dr:] = l_last

    wa = xl[:, :LANES]
    zw = w0_ref[...] + _dot(jnp.tanh(wa).astype(BF16), w2_ref[...])
    lw_s[...] = -jnp.exp(-0.5) * _sigmoid(zw)
    a_s[...] = _sigmoid(a0_ref[...] + _dot(wa.astype(BF16), a2_ref[...]))
    g_s[...] = _dot(_sigmoid(xl[:, LANES:]).astype(BF16), g2_ref[...])
    if has_vres:
        lo_rank = _dot(v.astype(BF16), v1_ref[...])
        mix = _sigmoid(v0_ref[...] + _dot(lo_rank.astype(BF16), v2_ref[...]))
        v = v + (vf_ref[...].astype(F32) - v) * mix
    r_s[...] = r
    k_s[...] = k
    v_s[...] = v
    vout_ref[...] = v.astype(vout_ref.dtype)
    _scan_kernel(r_s, k_s, v_s, a_s, g_s, lw_s, prm_ref, o_ref, state_ref)


def _rwkv_mix(p, prm, scan_prm, v_first, batch, seq):
    rows = SCAN_CHUNK * SCAN_CHUNKS_PER_STEP
    nc = seq // rows
    dr = RWKV_DIM
    has_vres = v_first is not None
    full = lambda a: pl.BlockSpec(a.shape, lambda b, c: (0,) * a.ndim)
    col = lambda cb: (lambda b, c: (b * nc + c, cb))
    in_specs = [pl.BlockSpec((rows, dr), col(COL_R // dr)),
                pl.BlockSpec((rows, dr), col(COL_K // dr)),
                pl.BlockSpec((rows, dr), col(COL_V // dr)),
                pl.BlockSpec((rows, LORA_COLS), col(COL_LORA // LORA_COLS))]
    args = [p] * 4
    small = [prm["mu_rkv"], prm["mu_lora"], prm["w0"], prm["a0"], prm["w2"], prm["a2"], prm["g2"]]
    if has_vres:
        small_v = [prm["v0"], prm["v1"], prm["v2"]]
        in_specs += [full(a) for a in small] + [pl.BlockSpec((rows, dr), col(0))] + [full(a) for a in small_v]
        args += small + [v_first] + small_v
    else:
        in_specs += [full(a) for a in small]
        args += small
    in_specs.append(full(scan_prm))
    args.append(scan_prm)
    blk = pl.BlockSpec((rows, dr), col(0))
    out = jax.ShapeDtypeStruct((batch * seq, dr), BF16)
    return pl.pallas_call(
        functools.partial(_rwkv_kernel, has_vres),
        grid=(batch, nc),
        in_specs=in_specs,
        out_specs=[blk, blk],
        out_shape=[out, out],
        scratch_shapes=[pltpu.VMEM((HEAD_PAIRS, 2 * RWKV_HEAD, LANES), F32),
                        pltpu.VMEM((SUBLANES, CARRY_COLS), F32)] + [pltpu.VMEM((rows, dr), F32)] * 6,
        compiler_params=_cparams(2),
        name="rwkv_mix",
    )(*args)


def _merge_kernel(alpha, z_ref, ya_ref, ga_ref, gb_ref, x_ref, wr_ref, wo_ref, lg_ref, lb_ref, o_ref, ot_ref):
    y_r = _dot(z_ref[...], wr_ref[...])
    mix = _sigmoid(ga_ref[...].astype(F32)) * ya_ref[...].astype(F32) + _sigmoid(gb_ref[...].astype(F32)) * y_r
    mixed = _dot(mix.astype(BF16), wo_ref[...])
    out = _layer_norm(alpha * x_ref[...] + mixed, lg_ref[...], lb_ref[...])
    o_ref[...] = out
    _store_token_rows(ot_ref, out)


def _merge(z, y_attn, p, x, w_r, w_o, ln_g, ln_b, alpha, tm):
    t, d = x.shape
    row = lambda i: (i, 0)
    full = lambda a: pl.BlockSpec(a.shape, lambda i: (0,) * a.ndim)
    return pl.pallas_call(
        functools.partial(_merge_kernel, alpha),
        grid=(t // tm,),
        in_specs=[pl.BlockSpec((tm, d), row), pl.BlockSpec((tm, d), row),
                  pl.BlockSpec((tm, d), lambda i: (i, COL_GA // d)),
                  pl.BlockSpec((tm, d), lambda i: (i, COL_GB // d)),
                  pl.BlockSpec((tm, d), row), full(w_r), full(w_o), full(ln_g), full(ln_b)],
        out_specs=[pl.BlockSpec((tm, d), row), pl.BlockSpec((tm * SUBLANES, LANES), row)],
        out_shape=[jax.ShapeDtypeStruct((t, d), F32), jax.ShapeDtypeStruct((t * SUBLANES, LANES), F32)],
        compiler_params=_cparams(1),
        name="merge_ln1",
    )(z, y_attn, p, p, x, w_r, w_o, ln_g, ln_b)


def _split3(x):
    hi = x.astype(BF16)
    rem = x - hi.astype(F32)
    mid = rem.astype(BF16)
    low = (rem - mid.astype(F32)).astype(BF16)
    return hi, mid, low


def _router_kernel(x_ref, w_ref, b_ref, idx_ref, gate_ref, rank_ref, cnt_ref, carry_ref):
    @pl.when(pl.program_id(0) == 0)
    def _():
        carry_ref[...] = jnp.zeros_like(carry_ref)

    x0, x1, x2 = _split3(x_ref[...])
    w0, w1, w2 = w_ref[0], w_ref[1], w_ref[2]
    logits = (_dot(x0, w0) + (_dot(x0, w1) + _dot(x1, w0)) +
              (_dot(x0, w2) + _dot(x1, w1) + _dot(x2, w0))) + b_ref[...]
    tm = logits.shape[0]
    lane = lax.broadcasted_iota(jnp.int32, (tm, LANES), 1).astype(F32)
    work = logits
    idx_out = jnp.zeros((tm, LANES), F32)
    val_out = jnp.zeros((tm, LANES), F32)
    top = None
    den = jnp.zeros((tm, 1), F32)
    hits = []
    for kk in range(TOP_K):
        m = jnp.max(work, axis=-1, keepdims=True)
        sel = jnp.min(jnp.where(work == m, lane, float(LANES)), axis=-1, keepdims=True)
        if kk == 0:
            top = m
        e = jnp.exp(m - top)
        den = den + e
        idx_out = jnp.where(lane == float(kk), sel, idx_out)
        val_out = jnp.where(lane == float(kk), e, val_out)
        hits.append(lane == sel)
        work = jnp.where(hits[-1], -jnp.inf, work)
    idx_ref[...] = idx_out.astype(jnp.int32)
    gate_ref[...] = val_out / den

    chosen = jnp.where(hits[0] | hits[1] | hits[2] | hits[3], 1.0, 0.0)
    r_i = lax.broadcasted_iota(jnp.int32, (tm, tm), 0)
    c_i = lax.broadcasted_iota(jnp.int32, (tm, tm), 1)
    earlier = jnp.where(r_i > c_i, 1.0, 0.0).astype(BF16)
    before = carry_ref[0:1, :] + _dot(earlier, chosen.astype(BF16))
    rank_out = jnp.zeros((tm, LANES), F32)
    for kk in range(TOP_K):
        rk = jnp.sum(jnp.where(hits[kk], before, 0.0), axis=-1, keepdims=True)
        rank_out = jnp.where(lane == float(kk), rk, rank_out)
    rank_ref[...] = rank_out.astype(jnp.int32)
    carry_ref[0:1, :] = carry_ref[0:1, :] + jnp.sum(chosen, axis=0, keepdims=True)
    cnt_ref[...] = carry_ref[...].astype(jnp.int32)


def _router(x, w3, b, tm):
    t, d = x.shape
    row = lambda i: (i, 0)
    return pl.pallas_call(
        _router_kernel,
        grid=(t // tm,),
        in_specs=[pl.BlockSpec((tm, d), row), pl.BlockSpec(w3.shape, lambda i: (0, 0, 0)),
                  pl.BlockSpec(b.shape, lambda i: (0, 0))],
        out_specs=[pl.BlockSpec((tm, LANES), row)] * 3 + [pl.BlockSpec((8, LANES), lambda i: (0, 0))],
        out_shape=[jax.ShapeDtypeStruct((t, LANES), jnp.int32), jax.ShapeDtypeStruct((t, LANES), F32),
                   jax.ShapeDtypeStruct((t, LANES), jnp.int32), jax.ShapeDtypeStruct((8, LANES), jnp.int32)],
        scratch_shapes=[pltpu.VMEM((8, LANES), F32)],
        compiler_params=_cparams(1),
        name="router",
    )(x, w3, b)


SUBLANES = 8
TOKEN_TILES = D_MODEL // LANES


def _row_copy(src_hbm, dst_vmem, sem, src_row8, dst_tok):
    if not isinstance(src_row8, int):
        src_row8 = pl.multiple_of(src_row8, SUBLANES)
    return pltpu.make_async_copy(src_hbm.at[pl.ds(src_row8, SUBLANES), :],
                                 dst_vmem.at[pl.ds(dst_tok * SUBLANES, SUBLANES), :], sem)


def _load_token_rows(ref, n):
    return jnp.concatenate([ref[pl.ds(s, n, stride=SUBLANES), :] for s in range(TOKEN_TILES)], axis=1)


def _store_token_rows(ref, val):
    n = val.shape[0]
    for s in range(TOKEN_TILES):
        ref[pl.ds(s, n, stride=SUBLANES), :] = val[:, s * LANES:(s + 1) * LANES]


DEINT_GROUP = 2 * LANES


def _deint_kernel(w_ref, sel_ref, g_ref, l_ref):
    for c in range(w_ref.shape[1] // DEINT_GROUP):
        blk = w_ref[:, c * DEINT_GROUP:(c + 1) * DEINT_GROUP].astype(BF16)
        both = _dot(blk, sel_ref[...])
        g_ref[:, c * LANES:(c + 1) * LANES] = both[:, :LANES].astype(BF16)
        l_ref[:, c * LANES:(c + 1) * LANES] = both[:, LANES:].astype(BF16)


def _deinterleave(w, tm):
    rows, cols = w.shape
    src = jnp.arange(DEINT_GROUP)[:, None]
    dst = jnp.arange(DEINT_GROUP)[None, :]
    sel = jnp.where(dst < LANES, src == 2 * dst, src == 2 * (dst - LANES) + 1).astype(BF16)
    half = jax.ShapeDtypeStruct((rows, cols // 2), BF16)
    return pl.pallas_call(
        _deint_kernel,
        grid=(rows // tm,),
        in_specs=[pl.BlockSpec((tm, cols), lambda i: (i, 0)), pl.BlockSpec(sel.shape, lambda i: (0, 0))],
        out_specs=[pl.BlockSpec((tm, cols // 2), lambda i: (i, 0))] * 2,
        out_shape=[half, half],
        compiler_params=_cparams(1),
        name="deinterleave",
    )(w, sel)


def _expert_kernel(be_ref, nu_ref, idx0_ref, idxn_ref, x_hbm, w1g_ref, w1l_ref, b1g_ref, b1l_ref, w2_ref, b2_ref,
                   o_ref, xbuf, sems):
    i = pl.program_id(0)
    last = pl.num_programs(0) - 1
    slot = lax.rem(i, 2)
    used = i < nu_ref[0]

    def issue(idx_ref, s):
        for r in range(MOE_BLOCK):
            _row_copy(x_hbm, xbuf.at[s], sems.at[s], idx_ref[0, 0, r], r).start(priority=r % 2)

    def drain(s):
        for r in range(MOE_BLOCK):
            _row_copy(x_hbm, xbuf.at[s], sems.at[s], 0, r).wait()

    @pl.when(i == 0)
    def _():
        issue(idx0_ref, 0)

    issue(idxn_ref, 1 - slot)
    drain(slot)

    @pl.when(i == last)
    def _():
        drain(1 - slot)

    @pl.when(jnp.logical_not(used))
    def _():
        o_ref[...] = jnp.zeros_like(o_ref)

    @pl.when(used)
    def _():
        xb = _load_token_rows(xbuf.at[slot], MOE_BLOCK).astype(BF16)
        h_glu = _dot(xb, w1g_ref[...]) + b1g_ref[...]
        h_lin = _dot(xb, w1l_ref[...]) + b1l_ref[...]
        x_glu = jnp.minimum(h_glu, SWIGLU_LIMIT)
        x_lin = jnp.clip(h_lin, -SWIGLU_LIMIT, SWIGLU_LIMIT)
        act = x_glu * _sigmoid(SWIGLU_ALPHA * x_glu) * (x_lin + 1.0)
        _store_token_rows(o_ref, _dot(act.astype(BF16), w2_ref[...]) + b2_ref[...])


def _experts(x_tiles, row_tok, block_exp, n_used, w1g, w1l, b1g, b1l, w2, b2):
    d = D_MODEL
    nblk = row_tok.shape[0] // MOE_BLOCK
    de = w1g.shape[2]
    idx = (row_tok * SUBLANES).reshape(nblk, 1, MOE_BLOCK)
    blk_rows = MOE_BLOCK * SUBLANES
    wmap = lambda i, be, nu: (be[i], 0, 0)
    grid_spec = pltpu.PrefetchScalarGridSpec(
        num_scalar_prefetch=2,
        grid=(nblk,),
        in_specs=[pl.BlockSpec((1, 1, MOE_BLOCK), lambda i, be, nu: (0, 0, 0), memory_space=pltpu.SMEM),
                  pl.BlockSpec((1, 1, MOE_BLOCK), lambda i, be, nu: (jnp.minimum(i + 1, nblk - 1), 0, 0),
                               memory_space=pltpu.SMEM),
                  pl.BlockSpec(memory_space=pl.ANY),
                  pl.BlockSpec((None, d, de), wmap), pl.BlockSpec((None, d, de), wmap),
                  pl.BlockSpec((None, 1, de), wmap), pl.BlockSpec((None, 1, de), wmap),
                  pl.BlockSpec((None, de, d), wmap), pl.BlockSpec((None, 1, d), wmap)],
        out_specs=pl.BlockSpec((blk_rows, LANES), lambda i, be, nu: (i, 0)),
        scratch_shapes=[pltpu.VMEM((2, blk_rows, LANES), F32), pltpu.SemaphoreType.DMA((2,))],
    )
    return pl.pallas_call(
        _expert_kernel,
        grid_spec=grid_spec,
        out_shape=jax.ShapeDtypeStruct((nblk * blk_rows, LANES), F32),
        compiler_params=_cparams(1),
        name="moe_experts",
    )(block_exp, n_used, idx, idx, x_tiles, w1g, w1l, b1g, b1l, w2, b2)


def _combine_kernel(alpha, idx0_ref, idxn_ref, y_hbm, gate_ref, x_ref, lg_ref, lb_ref, o_ref, buf, sems):
    tc = x_ref.shape[0]
    i = pl.program_id(0)
    last = pl.num_programs(0) - 1
    slot = lax.rem(i, 2)

    def issue(idx_ref, s):
        for r in range(tc):
            for kk in range(TOP_K):
                _row_copy(y_hbm, buf.at[s, kk], sems.at[s], idx_ref[0, 0, r * TOP_K + kk], r).start(priority=kk % 2)

    def drain(s):
        for r in range(tc * TOP_K):
            _row_copy(y_hbm, buf.at[s, 0], sems.at[s], 0, 0).wait()

    @pl.when(i == 0)
    def _():
        issue(idx0_ref, 0)

    issue(idxn_ref, 1 - slot)
    drain(slot)

    @pl.when(i == last)
    def _():
        drain(1 - slot)

    gates = gate_ref[...]
    y = _load_token_rows(buf.at[slot, 0], tc) * gates[:, 0:1]
    for kk in range(1, TOP_K):
        y = y + _load_token_rows(buf.at[slot, kk], tc) * gates[:, kk:kk + 1]
    o_ref[...] = _layer_norm(alpha * x_ref[...] + y, lg_ref[...], lb_ref[...])


def _combine(y_rows, dest, gates, x, ln_g, ln_b, alpha, tc):
    t, d = x.shape
    steps = t // tc
    row = lambda i: (i, 0)
    full = lambda a: pl.BlockSpec(a.shape, lambda i: (0,) * a.ndim)
    idx = (dest * SUBLANES).reshape(steps, 1, tc * TOP_K)
    return pl.pallas_call(
        functools.partial(_combine_kernel, alpha),
        grid=(steps,),
        in_specs=[pl.BlockSpec((1, 1, tc * TOP_K), lambda i: (0, 0, 0), memory_space=pltpu.SMEM),
                  pl.BlockSpec((1, 1, tc * TOP_K), lambda i: (jnp.minimum(i + 1, steps - 1), 0, 0),
                               memory_space=pltpu.SMEM),
                  pl.BlockSpec(memory_space=pl.ANY),
                  pl.BlockSpec((tc, LANES), row), pl.BlockSpec((tc, d), row), full(ln_g), full(ln_b)],
        out_specs=pl.BlockSpec((tc, d), row),
        out_shape=jax.ShapeDtypeStruct((t, d), F32),
        scratch_shapes=[pltpu.VMEM((2, TOP_K, tc * SUBLANES, LANES), F32), pltpu.SemaphoreType.DMA((2,))],
        compiler_params=_cparams(1),
        name="moe_combine_ln2",
    )(idx, idx, y_rows, gates, x, ln_g, ln_b)


def _routing_tables(top_idx, rank, counts):
    t = top_idx.shape[0]
    n_assign = t * TOP_K
    e_flat = top_idx.reshape(n_assign)
    padded = ((counts + MOE_BLOCK - 1) // MOE_BLOCK) * MOE_BLOCK
    pend = jnp.cumsum(padded)
    pstart = pend - padded
    dest = (pstart[e_flat] + rank.reshape(n_assign)).astype(jnp.int32)
    n_blocks = -(-n_assign // MOE_BLOCK) + N_EXPERTS
    tok_flat = jnp.arange(n_assign, dtype=jnp.int32) // TOP_K
    row_tok = jnp.zeros((n_blocks * MOE_BLOCK,), jnp.int32).at[dest].set(
        tok_flat, unique_indices=True, mode="promise_in_bounds")
    block_start = jnp.arange(n_blocks, dtype=jnp.int32) * MOE_BLOCK
    block_exp = jnp.minimum(jnp.sum(pend[None, :] <= block_start[:, None], axis=1), N_EXPERTS - 1).astype(jnp.int32)
    n_used = (pend[-1] // MOE_BLOCK).astype(jnp.int32).reshape(1)
    return dest, row_tok, block_exp, n_used


def _pad_rows(w, rows, at):
    out = jnp.zeros((rows, w.shape[1]), w.dtype)
    return out.at[at:at + w.shape[0]].set(w)


def _tile(n, pref):
    return pref if n % pref == 0 else n


def kernel(x, w_in, attn_sinks, w_attn_o, rwkv_mu, rwkv_w0, rwkv_w2, rwkv_a0, rwkv_a2, rwkv_g2, rwkv_k_k, rwkv_k_a, rwkv_r_k, rwkv_v0, rwkv_v1, rwkv_v2, rwkv_lnx_w, rwkv_lnx_b, w_rwkv_o, w_out, ln1_g, ln1_b, router_w, router_b, expert_w1, expert_b1, expert_w2, expert_b2, ln2_g, ln2_b):
    batch, seq, d = x.shape
    depth = w_in.shape[0]
    t = batch * seq
    alpha = (2.0 * depth) ** 0.25
    dr = RWKV_DIM
    tables = _rope_tables(seq)
    xf = x.reshape(t, d)
    v_first = None
    o_att, o_rw = ATT_Q + 2 * ATT_KV, ATT_Q + 2 * ATT_KV + 3 * dr + DECAY_LORA + AAA_LORA + GATE_LORA

    for l in range(depth):
        wl = w_in[l]
        wp = jnp.concatenate([
            wl[:, :ATT_Q], wl[:, o_att:o_att + 3 * dr], wl[:, o_rw:o_rw + 2 * d],
            wl[:, ATT_Q:o_att], wl[:, o_att + 3 * dr:o_rw]], axis=1).astype(BF16)
        p = _project(xf, wp, _tile(t, 512), P_COLS // 3)

        y_attn = _attention(p, attn_sinks[l], tables, w_attn_o[l].astype(BF16), batch, seq)

        mu = rwkv_mu[l]
        prm = {
            "mu_rkv": mu[:3 * dr].reshape(3, dr),
            "mu_lora": mu[3 * dr:].reshape(1, -1),
            "w0": rwkv_w0[l].reshape(1, dr), "a0": rwkv_a0[l].reshape(1, dr),
            "w2": _pad_rows(rwkv_w2[l], LANES, 0).astype(BF16),
            "a2": _pad_rows(rwkv_a2[l], LANES, DECAY_LORA).astype(BF16),
            "g2": rwkv_g2[l].astype(BF16),
        }
        if l > 0:
            prm["v0"] = rwkv_v0[l - 1].reshape(1, dr)
            prm["v1"] = jnp.zeros((dr, LANES), F32).at[:, :MV_LORA].set(rwkv_v1[l - 1]).astype(BF16)
            prm["v2"] = _pad_rows(rwkv_v2[l - 1], LANES, 0).astype(BF16)
        scan_prm = jnp.stack([rwkv_k_k[l], rwkv_k_a[l], rwkv_r_k[l].reshape(dr), rwkv_lnx_w[l], rwkv_lnx_b[l],
                              jnp.zeros((dr,), F32), jnp.zeros((dr,), F32), jnp.zeros((dr,), F32)])
        z, v_ = _rwkv_mix(p, prm, scan_prm, v_first, batch, seq)
        if l == 0:
            v_first = v_

        x1, x1_tiles = _merge(z, y_attn, p, xf, w_rwkv_o[l].astype(BF16), w_out[l].astype(BF16),
                    ln1_g[l].reshape(1, d), ln1_b[l].reshape(1, d), alpha, _tile(t, 512))

        rw = jnp.zeros((d, LANES), F32).at[:, :N_EXPERTS].set(router_w[l])
        rb = jnp.full((1, LANES), -1e30, F32).at[0, :N_EXPERTS].set(router_b[l])
        top_idx, gates, rank, counts = _router(x1, jnp.stack(_split3(rw)), rb, _tile(t, 512))
        dest, row_tok, block_exp, n_used = _routing_tables(top_idx[:, :TOP_K], rank[:, :TOP_K],
                                                           counts[0, :N_EXPERTS])

        b1 = expert_b1[l]
        w1g, w1l = _deinterleave(expert_w1[l].reshape(N_EXPERTS * d, 2 * D_EXPERT), 512)
        y_rows = _experts(x1_tiles, row_tok, block_exp, n_used,
                          w1g.reshape(N_EXPERTS, d, D_EXPERT), w1l.reshape(N_EXPERTS, d, D_EXPERT),
                          b1[:, None, 0::2], b1[:, None, 1::2],
                          expert_w2[l].astype(BF16), expert_b2[l][:, None, :])
        xf = _combine(y_rows, dest, gates, x1, ln2_g[l].reshape(1, d), ln2_b[l].reshape(1, d),
                      alpha, _tile(t, 128))
    return xf.reshape(batch, seq, d)
```

```python
import functools

import jax
import jax.numpy as jnp
from jax import lax
from jax.experimental import pallas as pl
from jax.experimental.pallas import tpu as pltpu

F32 = jnp.float32
BF16 = jnp.bfloat16

D_MODEL = 1024
HEAD_DIM = 64
N_Q_HEADS = 16
N_KV_HEADS = 4
WINDOW = 128
ATTN_BLOCK = 128
ROPE_THETA = 10000.0
ATT_Q = N_Q_HEADS * HEAD_DIM
ATT_KV = N_KV_HEADS * HEAD_DIM

RWKV_HEAD = 64
RWKV_HEADS = 16
RWKV_DIM = RWKV_HEADS * RWKV_HEAD
DECAY_LORA = 64
AAA_LORA = 64
MV_LORA = 32
GATE_LORA = 128
RWKV_GN_EPS = 64e-5

N_EXPERTS = 32
TOP_K = 4
D_EXPERT = 1024
SWIGLU_LIMIT = 7.0
SWIGLU_ALPHA = 1.702
MOE_BLOCK = 512

LN_EPS = 1e-5

LANES = 128
HEAD_PAIRS = RWKV_HEADS // 2
SCAN_CHUNK = 64
SCAN_CHUNKS_PER_STEP = 2
VMEM_LIMIT = 56 * 1024 * 1024

COL_Q, COL_R, COL_K, COL_V, COL_GA, COL_GB = 0, 1024, 2048, 3072, 4096, 5120
COL_AK, COL_AV, COL_LORA = 6144, 6400, 6656
P_COLS = 6912


def _cparams(n_axes):
    return pltpu.CompilerParams(dimension_semantics=("arbitrary",) * n_axes,
                                vmem_limit_bytes=VMEM_LIMIT)


def _dot(a, b):
    return jnp.dot(a, b, preferred_element_type=F32)


def _dot_nt(a, b):
    return lax.dot_general(a, b, (((1,), (1,)), ((), ())), preferred_element_type=F32)


def _dot_tn(a, b):
    return lax.dot_general(a, b, (((0,), (0,)), ((), ())), preferred_element_type=F32)


def _sigmoid(z):
    return 1.0 / (1.0 + jnp.exp(-z))


def _layer_norm(h, g, b):
    mu = jnp.mean(h, axis=-1, keepdims=True)
    c = h - mu
    var = jnp.mean(c * c, axis=-1, keepdims=True)
    return c * lax.rsqrt(var + LN_EPS) * g + b


def _proj_kernel(x_ref, w_ref, o_ref):
    o_ref[...] = _dot(x_ref[...].astype(BF16), w_ref[...]).astype(o_ref.dtype)


def _project(x, w, tm, tn):
    m, k = x.shape
    n = w.shape[1]
    return pl.pallas_call(
        _proj_kernel,
        grid=(n // tn, m // tm),
        in_specs=[pl.BlockSpec((tm, k), lambda j, i: (i, 0)),
                  pl.BlockSpec((k, tn), lambda j, i: (0, j))],
        out_specs=pl.BlockSpec((tm, tn), lambda j, i: (i, j)),
        out_shape=jax.ShapeDtypeStruct((m, n), BF16),
        compiler_params=_cparams(2),
        name="proj",
    )(x, w)


def _rope(x, c, s1, s2):
    return x * c + pltpu.roll(x, LANES - HEAD_DIM // 2, 1) * s1 + pltpu.roll(x, HEAD_DIM // 2, 1) * s2


def _attn_kernel(sink_ref, q_ref, kc_ref, kp_ref, vc_ref, vp_ref, tc_ref, tp_ref, wo_ref, o_ref):
    n = pl.program_id(1)
    blk = ATTN_BLOCK
    lane = lax.broadcasted_iota(jnp.int32, (1, LANES), 1)
    lo = lane < HEAD_DIM
    cq, s1q, s2q = tc_ref[0], tc_ref[1], tc_ref[2]
    cp, s1p, s2p = tp_ref[0], tp_ref[1], tp_ref[2]

    row = lax.broadcasted_iota(jnp.int32, (blk, 2 * blk), 0)
    col = lax.broadcasted_iota(jnp.int32, (blk, 2 * blk), 1)
    valid = (col > row) & (col <= row + WINDOW) & ((n > 0) | (col >= blk))

    def lo_hi(t, off):
        sw = pltpu.roll(t, HEAD_DIM, 1)
        t_lo = jnp.where(lo, t if off == 0 else sw, 0.0).astype(BF16)
        t_hi = jnp.where(lo, 0.0, sw if off == 0 else t).astype(BF16)
        return t_lo, t_hi

    kv_tiles = []
    for c in range(ATT_KV // LANES):
        sl = slice(c * LANES, (c + 1) * LANES)
        kc = _rope(kc_ref[:, sl].astype(F32), cq, s1q, s2q)
        kp = _rope(kp_ref[:, sl].astype(F32), cp, s1p, s2p)
        kt = jnp.concatenate([kp, kc], axis=0)
        vt = jnp.concatenate([vp_ref[:, sl].astype(F32), vc_ref[:, sl].astype(F32)], axis=0)
        kv_tiles.append((kt, vt))

    k_m, v_st = [], []
    for g in range(N_KV_HEADS):
        kt, vt = kv_tiles[g // 2]
        k_m.append(lo_hi(kt, g % 2))
        v_st.append(jnp.concatenate(lo_hi(vt, g % 2), axis=0))
    n_pairs = N_Q_HEADS // 2
    log2e = 1.4426950408889634
    qp = [(_rope(q_ref[:, j * LANES:(j + 1) * LANES].astype(F32), cq, s1q, s2q)
           * (HEAD_DIM ** -0.5 * log2e)).astype(BF16) for j in range(n_pairs)]
    scores = [_dot_nt(qp[hh // 2], k_m[hh // 4][hh % 2]) for hh in range(N_Q_HEADS)]
    probs, rden = [], []
    for hh in range(N_Q_HEADS):
        s = jnp.where(valid, scores[hh], -1e30)
        sink = sink_ref[hh] * log2e
        m = jnp.maximum(jnp.max(s, axis=-1, keepdims=True), sink)
        p = jnp.exp2(s - m)
        rden.append(1.0 / (jnp.sum(p, axis=-1, keepdims=True) + jnp.exp2(sink - m)))
        probs.append(p.astype(BF16))
    outs = [_dot(jnp.concatenate([probs[2 * j], probs[2 * j + 1]], axis=1), v_st[j // 2])
            * jnp.where(lo, rden[2 * j], rden[2 * j + 1]) for j in range(n_pairs)]
    att = jnp.concatenate(outs, axis=1).astype(BF16)
    o_ref[...] = _dot(att, wo_ref[...]).astype(o_ref.dtype)


def _attention(p, sinks, tables, w_o, batch, seq):
    nb = seq // ATTN_BLOCK
    blk = ATTN_BLOCK
    kcol, vcol = COL_AK // ATT_KV, COL_AV // ATT_KV
    cur = lambda b, n: b * nb + n
    prev = lambda b, n: b * nb + jnp.maximum(n - 1, 0)
    return pl.pallas_call(
        _attn_kernel,
        grid=(batch, nb),
        in_specs=[
            pl.BlockSpec(memory_space=pltpu.SMEM),
            pl.BlockSpec((blk, ATT_Q), lambda b, n: (cur(b, n), COL_Q // ATT_Q)),
            pl.BlockSpec((blk, ATT_KV), lambda b, n: (cur(b, n), kcol)),
            pl.BlockSpec((blk, ATT_KV), lambda b, n: (prev(b, n), kcol)),
            pl.BlockSpec((blk, ATT_KV), lambda b, n: (cur(b, n), vcol)),
            pl.BlockSpec((blk, ATT_KV), lambda b, n: (prev(b, n), vcol)),
            pl.BlockSpec((3, blk, LANES), lambda b, n: (0, n, 0)),
            pl.BlockSpec((3, blk, LANES), lambda b, n: (0, jnp.maximum(n - 1, 0), 0)),
            pl.BlockSpec((ATT_Q, D_MODEL), lambda b, n: (0, 0)),
        ],
        out_specs=pl.BlockSpec((blk, D_MODEL), lambda b, n: (cur(b, n), 0)),
        out_shape=jax.ShapeDtypeStruct((batch * seq, D_MODEL), BF16),
        compiler_params=_cparams(2),
        name="attn",
    )(sinks, p, p, p, p, p, tables, tables, w_o)


def _rope_tables(seq):
    pos = jnp.arange(seq, dtype=F32)
    inv = ROPE_THETA ** (-jnp.arange(0, HEAD_DIM, 2, dtype=F32) / HEAD_DIM)
    ang = pos[:, None] * inv[None, :]
    cos, sin = jnp.cos(ang), jnp.sin(ang)
    zero = jnp.zeros_like(sin)
    c = jnp.tile(cos, (1, LANES // (HEAD_DIM // 2)))
    s1 = jnp.tile(jnp.concatenate([-sin, zero], axis=1), (1, LANES // HEAD_DIM))
    s2 = jnp.tile(jnp.concatenate([zero, sin], axis=1), (1, LANES // HEAD_DIM))
    return jnp.stack([c, s1, s2])


def _scan_kernel(r_ref, k_ref, v_ref, a_ref, g_ref, lw_ref, prm_ref, o_ref, state_ref):
    ch = SCAN_CHUNK
    c_id = pl.program_id(1)

    @pl.when(c_id == 0)
    def _():
        state_ref[...] = jnp.zeros_like(state_ref)

    n_chunks = lw_ref.shape[0] // ch
    chunk_rows = [slice(cc * ch, (cc + 1) * ch) for cc in range(n_chunks)]
    tri = (lax.broadcasted_iota(jnp.int32, (ch, ch), 0) >= lax.broadcasted_iota(jnp.int32, (ch, ch), 1)
           ).astype(BF16)
    e_in, e_ex, e_inv, e_end, g_tot, e_ref = [], [], [], [], [], []
    for rows in chunk_rows:
        lw = lw_ref[rows, :]
        hi = lw.astype(BF16)
        rem = lw - hi.astype(F32)
        mid = rem.astype(BF16)
        low = (rem - mid.astype(F32)).astype(BF16)
        cum = _dot(tri, hi) + _dot(tri, mid) + _dot(tri, low)
        ref_row = cum[ch // 2 - 1:ch // 2, :]
        total = cum[ch - 1:ch, :]
        e_in.append(jnp.exp(cum - ref_row))
        e_ex.append(jnp.exp(cum - lw - ref_row))
        e_inv.append(jnp.exp(ref_row - cum))
        e_end.append(jnp.exp(total - cum))
        g_tot.append(jnp.exp(total))
        e_ref.append(jnp.exp(ref_row))

    lane = lax.broadcasted_iota(jnp.int32, (1, LANES), 1)
    lo = lane < RWKV_HEAD
    r2 = lax.broadcasted_iota(jnp.int32, (2 * ch, 2 * ch), 0)
    c2 = lax.broadcasted_iota(jnp.int32, (2 * ch, 2 * ch), 1)
    same = ((r2 < ch) & (c2 < ch)) | ((r2 >= ch) & (c2 >= ch))
    tri_strict = same & (r2 > c2)
    tri_incl = same & (r2 >= c2)
    eye = jnp.where(r2 == c2, 1.0, 0.0)
    rb = lax.broadcasted_iota(jnp.int32, (2 * ch, LANES), 0)
    cb = lax.broadcasted_iota(jnp.int32, (2 * ch, LANES), 1)
    blockmask = ((rb < ch) & (cb < RWKV_HEAD)) | ((rb >= ch) & (cb >= RWKV_HEAD))

    def stack(x):
        return jnp.concatenate([jnp.where(lo, x, 0.0), jnp.where(lo, 0.0, x)], axis=0)

    def head_sums(x):
        s_e = jnp.sum(jnp.where(lo, x, 0.0), axis=-1, keepdims=True)
        s_o = jnp.sum(jnp.where(lo, 0.0, x), axis=-1, keepdims=True)
        return jnp.where(lo, s_e, s_o)

    sls = [slice(pr * LANES, (pr + 1) * LANES) for pr in range(HEAD_PAIRS)]
    items = [(cc, pr) for cc in range(n_chunks) for pr in range(HEAD_PAIRS)]
    pairs = range(len(items))

    def kmod_of(rows, sl):
        return k_ref[rows, sl].astype(F32) * (1.0 + (a_ref[rows, sl].astype(F32) - 1.0) * prm_ref[1:2, sl])

    abar, rbar, vs, lhs, rhs, upd = [], [], [], [], [], []
    for cc, pr in items:
        rows, sl = chunk_rows[cc], sls[pr]
        a = a_ref[rows, sl].astype(F32)
        kx = k_ref[rows, sl].astype(F32) * prm_ref[0:1, sl]
        kk = kx / jnp.maximum(jnp.sqrt(head_sums(kx * kx)), 1e-12)
        beta = kk * a
        kmod = kmod_of(rows, sl)
        ab = stack(-kk * e_ex[cc][:, sl]).astype(BF16)
        rb_ = stack(r_ref[rows, sl].astype(F32) * e_in[cc][:, sl]).astype(BF16)
        abar.append(ab)
        rbar.append(rb_)
        lhs.append(jnp.concatenate([ab, rb_], axis=0))
        rhs.append(jnp.concatenate([stack(beta * e_inv[cc][:, sl]).astype(BF16),
                                    stack(kmod * e_inv[cc][:, sl]).astype(BF16)], axis=0))
        upd.append(jnp.concatenate([stack(beta * e_end[cc][:, sl]).astype(BF16),
                                    stack(kmod * e_end[cc][:, sl]).astype(BF16)], axis=0))
        vs.append(stack(v_ref[rows, sl].astype(F32)).astype(BF16))

    grams = [_dot_nt(lhs[pr], rhs[pr]) for pr in pairs]
    a_ak = [jnp.where(tri_strict, gm[:2 * ch, 2 * ch:], 0.0).astype(BF16) for gm in grams]
    m_r = [jnp.concatenate([jnp.where(tri_incl, gm[2 * ch:, :2 * ch], 0.0),
                            jnp.where(tri_incl, gm[2 * ch:, 2 * ch:], 0.0)], axis=1).astype(BF16) for gm in grams]

    a_ab = [jnp.where(tri_strict, gm[:2 * ch, :2 * ch], 0.0) for gm in grams]
    pw = [x.astype(BF16) for x in a_ab]
    pw = [_dot(x, x).astype(BF16) for x in pw]
    tinv = [eye + x for x in a_ab]
    for _ in range((ch - 1).bit_length() - 2):
        both = [_dot(pw[pr], jnp.concatenate([pw[pr], tinv[pr].astype(BF16)], axis=1)) for pr in pairs]
        tinv = [tinv[pr] + both[pr][:, 2 * ch:] for pr in pairs]
        pw = [both[pr][:, :2 * ch].astype(BF16) for pr in pairs]
    tinv = [tinv[pr] + _dot(pw[pr], tinv[pr].astype(BF16)) for pr in pairs]

    heads = range(HEAD_PAIRS)
    st = [state_ref[pr] for pr in heads]
    o_all = []
    for cc in range(n_chunks):
        it = [cc * HEAD_PAIRS + pr for pr in heads]
        st_b = [(st[pr] * e_ref[cc][:, sls[pr]]).astype(BF16) for pr in heads]
        w = [_dot_nt(abar[it[pr]], st_b[pr]) + _dot(a_ak[it[pr]], vs[it[pr]]) for pr in heads]
        u = [_dot(tinv[it[pr]].astype(BF16), w[pr].astype(BF16)) for pr in heads]
        uv = [jnp.concatenate([u[pr].astype(BF16), vs[it[pr]]], axis=0) for pr in heads]
        o_all.append([_dot_nt(rbar[it[pr]], st_b[pr]) + _dot(m_r[it[pr]], uv[pr]) for pr in heads])
        st = [st[pr] * g_tot[cc][:, sls[pr]] + _dot_tn(uv[pr], upd[it[pr]]) for pr in heads]
    for pr in heads:
        state_ref[pr] = st[pr]

    for cc, rows in enumerate(chunk_rows):
        for pr, sl in enumerate(sls):
            o_s = o_all[cc][pr]
            mean = jnp.sum(o_s, axis=-1, keepdims=True) * (1.0 / RWKV_HEAD)
            cen = jnp.where(blockmask, o_s - mean, 0.0)
            var = jnp.sum(cen * cen, axis=-1, keepdims=True) * (1.0 / RWKV_HEAD)
            gn_s = cen * lax.rsqrt(var + RWKV_GN_EPS)
            gn = gn_s[:ch] + gn_s[ch:]
            v = v_ref[rows, sl].astype(F32)
            bonus = head_sums(r_ref[rows, sl].astype(F32) * kmod_of(rows, sl) * prm_ref[2:3, sl]) * v
            z = (gn * prm_ref[3:4, sl] + prm_ref[4:5, sl] + bonus) * g_ref[rows, sl].astype(F32)
            o_ref[rows, sl] = z.astype(o_ref.dtype)


LORA_COLS = DECAY_LORA + AAA_LORA + GATE_LORA
CARRY_COLS = 3 * RWKV_DIM + LORA_COLS


def _rwkv_kernel(has_vres, *refs):
    if has_vres:
        (r_in, k_in, v_in, l_in, mu_ref, mul_ref, w0_ref, a0_ref, w2_ref, a2_ref, g2_ref,
         vf_ref, v0_ref, v1_ref, v2_ref, prm_ref,
         o_ref, vout_ref, state_ref, carry_ref, r_s, k_s, v_s, a_s, g_s, lw_s) = refs
    else:
        (r_in, k_in, v_in, l_in, mu_ref, mul_ref, w0_ref, a0_ref, w2_ref, a2_ref, g2_ref, prm_ref,
         o_ref, vout_ref, state_ref, carry_ref, r_s, k_s, v_s, a_s, g_s, lw_s) = refs
    rows = r_in.shape[0]
    dr = RWKV_DIM

    @pl.when(pl.program_id(1) == 0)
    def _():
        carry_ref[...] = jnp.zeros_like(carry_ref)

    row0 = lax.broadcasted_iota(jnp.int32, (rows, 1), 0) == 0

    def shift_mix(cur_ref, col, mu):
        cur = cur_ref[...].astype(F32)
        last = carry_ref[0:1, col:col + cur.shape[1]]
        sh = jnp.where(row0, last, pltpu.roll(cur, 1, 0))
        return cur + (sh - cur) * mu, cur[rows - 1:rows, :]

    r, r_last = shift_mix(r_in, 0, mu_ref[0:1, :])
    k, k_last = shift_mix(k_in, dr, mu_ref[1:2, :])
    v, v_last = shift_mix(v_in, 2 * dr, mu_ref[2:3, :])
    xl, l_last = shift_mix(l_in, 3 * dr, mul_ref[...])
    carry_ref[0:1, 0:dr] = r_last
    carry_ref[0:1, dr:2 * dr] = k_last
    carry_ref[0:1, 2 * dr:3 * dr] = v_last
    carry_ref[0:1, 3 * dr:] = l_last

    wa = xl[:, :LANES]
    zw = w0_ref[...] + _dot(jnp.tanh(wa).astype(BF16), w2_ref[...])
    lw_s[...] = -jnp.exp(-0.5) * _sigmoid(zw)
    a_s[...] = _sigmoid(a0_ref[...] + _dot(wa.astype(BF16), a2_ref[...]))
    g_s[...] = _dot(_sigmoid(xl[:, LANES:]).astype(BF16), g2_ref[...])
    if has_vres:
        lo_rank = _dot(v.astype(BF16), v1_ref[...])
        mix = _sigmoid(v0_ref[...] + _dot(lo_rank.astype(BF16), v2_ref[...]))
        v = v + (vf_ref[...].astype(F32) - v) * mix
    r_s[...] = r
    k_s[...] = k
    v_s[...] = v
    vout_ref[...] = v.astype(vout_ref.dtype)
    _scan_kernel(r_s, k_s, v_s, a_s, g_s, lw_s, prm_ref, o_ref, state_ref)


def _rwkv_mix(p, prm, scan_prm, v_first, batch, seq):
    rows = SCAN_CHUNK * SCAN_CHUNKS_PER_STEP
    nc = seq // rows
    dr = RWKV_DIM
    has_vres = v_first is not None
    full = lambda a: pl.BlockSpec(a.shape, lambda b, c: (0,) * a.ndim)
    col = lambda cb: (lambda b, c: (b * nc + c, cb))
    in_specs = [pl.BlockSpec((rows, dr), col(COL_R // dr)),
                pl.BlockSpec((rows, dr), col(COL_K // dr)),
                pl.BlockSpec((rows, dr), col(COL_V // dr)),
                pl.BlockSpec((rows, LORA_COLS), col(COL_LORA // LORA_COLS))]
    args = [p] * 4
    small = [prm["mu_rkv"], prm["mu_lora"], prm["w0"], prm["a0"], prm["w2"], prm["a2"], prm["g2"]]
    if has_vres:
        small_v = [prm["v0"], prm["v1"], prm["v2"]]
        in_specs += [full(a) for a in small] + [pl.BlockSpec((rows, dr), col(0))] + [full(a) for a in small_v]
        args += small + [v_first] + small_v
    else:
        in_specs += [full(a) for a in small]
        args += small
    in_specs.append(full(scan_prm))
    args.append(scan_prm)
    blk = pl.BlockSpec((rows, dr), col(0))
    out = jax.ShapeDtypeStruct((batch * seq, dr), BF16)
    return pl.pallas_call(
        functools.partial(_rwkv_kernel, has_vres),
        grid=(batch, nc),
        in_specs=in_specs,
        out_specs=[blk, blk],
        out_shape=[out, out],
        scratch_shapes=[pltpu.VMEM((HEAD_PAIRS, 2 * RWKV_HEAD, LANES), F32),
                        pltpu.VMEM((SUBLANES, CARRY_COLS), F32)] + [pltpu.VMEM((rows, dr), F32)] * 6,
        compiler_params=_cparams(2),
        name="rwkv_mix",
    )(*args)


def _merge_kernel(alpha, z_ref, ya_ref, ga_ref, gb_ref, x_ref, wr_ref, wo_ref, lg_ref, lb_ref, o_ref, ot_ref):
    y_r = _dot(z_ref[...], wr_ref[...])
    mix = _sigmoid(ga_ref[...].astype(F32)) * ya_ref[...].astype(F32) + _sigmoid(gb_ref[...].astype(F32)) * y_r
    mixed = _dot(mix.astype(BF16), wo_ref[...])
    out = _layer_norm(alpha * x_ref[...] + mixed, lg_ref[...], lb_ref[...])
    o_ref[...] = out
    _store_token_rows(ot_ref, out)


def _merge(z, y_attn, p, x, w_r, w_o, ln_g, ln_b, alpha, tm):
    t, d = x.shape
    row = lambda i: (i, 0)
    full = lambda a: pl.BlockSpec(a.shape, lambda i: (0,) * a.ndim)
    return pl.pallas_call(
        functools.partial(_merge_kernel, alpha),
        grid=(t // tm,),
        in_specs=[pl.BlockSpec((tm, d), row), pl.BlockSpec((tm, d), row),
                  pl.BlockSpec((tm, d), lambda i: (i, COL_GA // d)),
                  pl.BlockSpec((tm, d), lambda i: (i, COL_GB // d)),
                  pl.BlockSpec((tm, d), row), full(w_r), full(w_o), full(ln_g), full(ln_b)],
        out_specs=[pl.BlockSpec((tm, d), row), pl.BlockSpec((tm * SUBLANES, LANES), row)],
        out_shape=[jax.ShapeDtypeStruct((t, d), F32), jax.ShapeDtypeStruct((t * SUBLANES, LANES), F32)],
        compiler_params=_cparams(1),
        name="merge_ln1",
    )(z, y_attn, p, p, x, w_r, w_o, ln_g, ln_b)


def _split3(x):
    hi = x.astype(BF16)
    rem = x - hi.astype(F32)
    mid = rem.astype(BF16)
    low = (rem - mid.astype(F32)).astype(BF16)
    return hi, mid, low


def _router_kernel(x_ref, w_ref, b_ref, idx_ref, gate_ref, rank_ref, cnt_ref, carry_ref):
    @pl.when(pl.program_id(0) == 0)
    def _():
        carry_ref[...] = jnp.zeros_like(carry_ref)

    x0, x1, x2 = _split3(x_ref[...])
    w0, w1, w2 = w_ref[0], w_ref[1], w_ref[2]
    logits = (_dot(x0, w0) + (_dot(x0, w1) + _dot(x1, w0)) +
              (_dot(x0, w2) + _dot(x1, w1) + _dot(x2, w0))) + b_ref[...]
    tm = logits.shape[0]
    lane = lax.broadcasted_iota(jnp.int32, (tm, LANES), 1).astype(F32)
    work = logits
    idx_out = jnp.zeros((tm, LANES), F32)
    val_out = jnp.zeros((tm, LANES), F32)
    top = None
    den = jnp.zeros((tm, 1), F32)
    hits = []
    for kk in range(TOP_K):
        m = jnp.max(work, axis=-1, keepdims=True)
        sel = jnp.min(jnp.where(work == m, lane, float(LANES)), axis=-1, keepdims=True)
        if kk == 0:
            top = m
        e = jnp.exp(m - top)
        den = den + e
        idx_out = jnp.where(lane == float(kk), sel, idx_out)
        val_out = jnp.where(lane == float(kk), e, val_out)
        hits.append(lane == sel)
        work = jnp.where(hits[-1], -jnp.inf, work)
    idx_ref[...] = idx_out.astype(jnp.int32)
    gate_ref[...] = val_out / den

    chosen = jnp.where(hits[0] | hits[1] | hits[2] | hits[3], 1.0, 0.0)
    r_i = lax.broadcasted_iota(jnp.int32, (tm, tm), 0)
    c_i = lax.broadcasted_iota(jnp.int32, (tm, tm), 1)
    earlier = jnp.where(r_i > c_i, 1.0, 0.0).astype(BF16)
    before = carry_ref[0:1, :] + _dot(earlier, chosen.astype(BF16))
    rank_out = jnp.zeros((tm, LANES), F32)
    for kk in range(TOP_K):
        rk = jnp.sum(jnp.where(hits[kk], before, 0.0), axis=-1, keepdims=True)
        rank_out = jnp.where(lane == float(kk), rk, rank_out)
    rank_ref[...] = rank_out.astype(jnp.int32)
    carry_ref[0:1, :] = carry_ref[0:1, :] + jnp.sum(chosen, axis=0, keepdims=True)
    cnt_ref[...] = carry_ref[...].astype(jnp.int32)


def _router(x, w3, b, tm):
    t, d = x.shape
    row = lambda i: (i, 0)
    return pl.pallas_call(
        _router_kernel,
        grid=(t // tm,),
        in_specs=[pl.BlockSpec((tm, d), row), pl.BlockSpec(w3.shape, lambda i: (0, 0, 0)),
                  pl.BlockSpec(b.shape, lambda i: (0, 0))],
        out_specs=[pl.BlockSpec((tm, LANES), row)] * 3 + [pl.BlockSpec((8, LANES), lambda i: (0, 0))],
        out_shape=[jax.ShapeDtypeStruct((t, LANES), jnp.int32), jax.ShapeDtypeStruct((t, LANES), F32),
                   jax.ShapeDtypeStruct((t, LANES), jnp.int32), jax.ShapeDtypeStruct((8, LANES), jnp.int32)],
        scratch_shapes=[pltpu.VMEM((8, LANES), F32)],
        compiler_params=_cparams(1),
        name="router",
    )(x, w3, b)


SUBLANES = 8
TOKEN_TILES = D_MODEL // LANES


def _row_copy(src_hbm, dst_vmem, sem, src_row8, dst_tok):
    if not isinstance(src_row8, int):
        src_row8 = pl.multiple_of(src_row8, SUBLANES)
    return pltpu.make_async_copy(src_hbm.at[pl.ds(src_row8, SUBLANES), :],
                                 dst_vmem.at[pl.ds(dst_tok * SUBLANES, SUBLANES), :], sem)


def _load_token_rows(ref, n):
    return jnp.concatenate([ref[pl.ds(s, n, stride=SUBLANES), :] for s in range(TOKEN_TILES)], axis=1)


def _store_token_rows(ref, val):
    n = val.shape[0]
    for s in range(TOKEN_TILES):
        ref[pl.ds(s, n, stride=SUBLANES), :] = val[:, s * LANES:(s + 1) * LANES]


DEINT_GROUP = 2 * LANES

def _deint_kernel(w_ref, sel_ref, g_ref, l_ref):
    for c in range(w_ref.shape[1] // DEINT_GROUP):
        blk = w_ref[:, c * DEINT_GROUP:(c + 1) * DEINT_GROUP].astype(BF16)
        both = _dot(blk, sel_ref[...])
        g_ref[:, c * LANES:(c + 1) * LANES] = both[:, :LANES].astype(BF16)
        l_ref[:, c * LANES:(c + 1) * LANES] = both[:, LANES:].astype(BF16)


def _deinterleave(w, tm):
    rows, cols = w.shape
    src = jnp.arange(DEINT_GROUP)[:, None]
    dst = jnp.arange(DEINT_GROUP)[None, :]
    sel = jnp.where(dst < LANES, src == 2 * dst, src == 2 * (dst - LANES) + 1).astype(BF16)
    half = jax.ShapeDtypeStruct((rows, cols // 2), BF16)
    return pl.pallas_call(
        _deint_kernel,
        grid=(rows // tm,),
        in_specs=[pl.BlockSpec((tm, cols), lambda i: (i, 0)), pl.BlockSpec(sel.shape, lambda i: (0, 0))],
        out_specs=[pl.BlockSpec((tm, cols // 2), lambda i: (i, 0))] * 2,
        out_shape=[half, half],
        compiler_params=_cparams(1),
        name="deinterleave",
    )(w, sel)


def _expert_kernel(be_ref, nu_ref, idx0_ref, idxn_ref, x_hbm, w1g_ref, w1l_ref, b1g_ref, b1l_ref, w2_ref, b2_ref,
                   o_ref, xbuf, sems):
    i = pl.program_id(0)
    last = pl.num_programs(0) - 1
    slot = lax.rem(i, 2)
    used = i < nu_ref[0]

    def issue(idx_ref, s):
        for r in range(MOE_BLOCK):
            _row_copy(x_hbm, xbuf.at[s], sems.at[s], idx_ref[0, 0, r], r).start(priority=r % 2)

    def drain(s):
        for r in range(MOE_BLOCK):
            _row_copy(x_hbm, xbuf.at[s], sems.at[s], 0, r).wait()

    @pl.when(i == 0)
    def _():
        issue(idx0_ref, 0)

    issue(idxn_ref, 1 - slot)
    drain(slot)

    @pl.when(i == last)
    def _():
        drain(1 - slot)

    @pl.when(jnp.logical_not(used))
    def _():
        o_ref[...] = jnp.zeros_like(o_ref)

    @pl.when(used)
    def _():
        xb = _load_token_rows(xbuf.at[slot], MOE_BLOCK).astype(BF16)
        h_glu = _dot(xb, w1g_ref[...]) + b1g_ref[...]
        h_lin = _dot(xb, w1l_ref[...]) + b1l_ref[...]
        x_glu = jnp.minimum(h_glu, SWIGLU_LIMIT)
        x_lin = jnp.clip(h_lin, -SWIGLU_LIMIT, SWIGLU_LIMIT)
        act = x_glu * _sigmoid(SWIGLU_ALPHA * x_glu) * (x_lin + 1.0)
        _store_token_rows(o_ref, _dot(act.astype(BF16), w2_ref[...]) + b2_ref[...])


def _experts(x_tiles, row_tok, block_exp, n_used, w1g, w1l, b1g, b1l, w2, b2):
    d = D_MODEL
    nblk = row_tok.shape[0] // MOE_BLOCK
    de = w1g.shape[2]
    idx = (row_tok * SUBLANES).reshape(nblk, 1, MOE_BLOCK)
    blk_rows = MOE_BLOCK * SUBLANES
    wmap = lambda i, be, nu: (be[i], 0, 0)
    grid_spec = pltpu.PrefetchScalarGridSpec(
        num_scalar_prefetch=2,
        grid=(nblk,),
        in_specs=[pl.BlockSpec((1, 1, MOE_BLOCK), lambda i, be, nu: (0, 0, 0), memory_space=pltpu.SMEM),
                  pl.BlockSpec((1, 1, MOE_BLOCK), lambda i, be, nu: (jnp.minimum(i + 1, nblk - 1), 0, 0),
                               memory_space=pltpu.SMEM),
                  pl.BlockSpec(memory_space=pl.ANY),
                  pl.BlockSpec((None, d, de), wmap), pl.BlockSpec((None, d, de), wmap),
                  pl.BlockSpec((None, 1, de), wmap), pl.BlockSpec((None, 1, de), wmap),
                  pl.BlockSpec((None, de, d), wmap), pl.BlockSpec((None, 1, d), wmap)],
        out_specs=pl.BlockSpec((blk_rows, LANES), lambda i, be, nu: (i, 0)),
        scratch_shapes=[pltpu.VMEM((2, blk_rows, LANES), F32), pltpu.SemaphoreType.DMA((2,))],
    )
    return pl.pallas_call(
        _expert_kernel,
        grid_spec=grid_spec,
        out_shape=jax.ShapeDtypeStruct((nblk * blk_rows, LANES), F32),
        compiler_params=_cparams(1),
        name="moe_experts",
    )(block_exp, n_used, idx, idx, x_tiles, w1g, w1l, b1g, b1l, w2, b2)


def _combine_kernel(alpha, idx0_ref, idxn_ref, y_hbm, gate_ref, x_ref, lg_ref, lb_ref, o_ref, buf, sems):
    tc = x_ref.shape[0]
    i = pl.program_id(0)
    last = pl.num_programs(0) - 1
    slot = lax.rem(i, 2)

    def issue(idx_ref, s):
        for r in range(tc):
            for kk in range(TOP_K):
                _row_copy(y_hbm, buf.at[s, kk], sems.at[s], idx_ref[0, 0, r * TOP_K + kk], r).start(priority=kk % 2)

    def drain(s):
        for r in range(tc * TOP_K):
            _row_copy(y_hbm, buf.at[s, 0], sems.at[s], 0, 0).wait()

    @pl.when(i == 0)
    def _():
        issue(idx0_ref, 0)

    issue(idxn_ref, 1 - slot)
    drain(slot)

    @pl.when(i == last)
    def _():
        drain(1 - slot)

    gates = gate_ref[...]
    y = _load_token_rows(buf.at[slot, 0], tc) * gates[:, 0:1]
    for kk in range(1, TOP_K):
        y = y + _load_token_rows(buf.at[slot, kk], tc) * gates[:, kk:kk + 1]
    o_ref[...] = _layer_norm(alpha * x_ref[...] + y, lg_ref[...], lb_ref[...])


def _combine(y_rows, dest, gates, x, ln_g, ln_b, alpha, tc):
    t, d = x.shape
    steps = t // tc
    row = lambda i: (i, 0)
    full = lambda a: pl.BlockSpec(a.shape, lambda i: (0,) * a.ndim)
    idx = (dest * SUBLANES).reshape(steps, 1, tc * TOP_K)
    return pl.pallas_call(
        functools.partial(_combine_kernel, alpha),
        grid=(steps,),
        in_specs=[pl.BlockSpec((1, 1, tc * TOP_K), lambda i: (0, 0, 0), memory_space=pltpu.SMEM),
                  pl.BlockSpec((1, 1, tc * TOP_K), lambda i: (jnp.minimum(i + 1, steps - 1), 0, 0),
                               memory_space=pltpu.SMEM),
                  pl.BlockSpec(memory_space=pl.ANY),
                  pl.BlockSpec((tc, LANES), row), pl.BlockSpec((tc, d), row), full(ln_g), full(ln_b)],
        out_specs=pl.BlockSpec((tc, d), row),
        out_shape=jax.ShapeDtypeStruct((t, d), F32),
        scratch_shapes=[pltpu.VMEM((2, TOP_K, tc * SUBLANES, LANES), F32), pltpu.SemaphoreType.DMA((2,))],
        compiler_params=_cparams(1),
        name="moe_combine_ln2",
    )(idx, idx, y_rows, gates, x, ln_g, ln_b)


def _routing_tables(top_idx, rank, counts):
    t = top_idx.shape[0]
    n_assign = t * TOP_K
    e_flat = top_idx.reshape(n_assign)
    padded = ((counts + MOE_BLOCK - 1) // MOE_BLOCK) * MOE_BLOCK
    pend = jnp.cumsum(padded)
    pstart = pend - padded
    dest = (pstart[e_flat] + rank.reshape(n_assign)).astype(jnp.int32)
    n_blocks = -(-n_assign // MOE_BLOCK) + N_EXPERTS
    tok_flat = jnp.arange(n_assign, dtype=jnp.int32) // TOP_K
    row_tok = jnp.zeros((n_blocks * MOE_BLOCK,), jnp.int32).at[dest].set(
        tok_flat, unique_indices=True, mode="promise_in_bounds")
    block_start = jnp.arange(n_blocks, dtype=jnp.int32) * MOE_BLOCK
    block_exp = jnp.minimum(jnp.sum(pend[None, :] <= block_start[:, None], axis=1), N_EXPERTS - 1).astype(jnp.int32)
    n_used = (pend[-1] // MOE_BLOCK).astype(jnp.int32).reshape(1)
    return dest, row_tok, block_exp, n_used


def _pad_rows(w, rows, at):
    out = jnp.zeros((rows, w.shape[1]), w.dtype)
    return out.at[at:at + w.shape[0]].set(w)


def _tile(n, pref):
    return pref if n % pref == 0 else n


def kernel(x, w_in, attn_sinks, w_attn_o, rwkv_mu, rwkv_w0, rwkv_w2, rwkv_a0, rwkv_a2, rwkv_g2, rwkv_k_k, rwkv_k_a, rwkv_r_k, rwkv_v0, rwkv_v1, rwkv_v2, rwkv_lnx_w, rwkv_lnx_b, w_rwkv_o, w_out, ln1_g, ln1_b, router_w, router_b, expert_w1, expert_b1, expert_w2, expert_b2, ln2_g, ln2_b):
    batch, seq, d = x.shape
    depth = w_in.shape[0]
    t = batch * seq
    alpha = (2.0 * depth) ** 0.25
    dr = RWKV_DIM
    tables = _rope_tables(seq)
    xf = x.reshape(t, d)
    v_first = None
    o_att, o_rw = ATT_Q + 2 * ATT_KV, ATT_Q + 2 * ATT_KV + 3 * dr + DECAY_LORA + AAA_LORA + GATE_LORA

    for l in range(depth):
        wl = w_in[l]
        wp = jnp.concatenate([
            wl[:, :ATT_Q], wl[:, o_att:o_att + 3 * dr], wl[:, o_rw:o_rw + 2 * d],
            wl[:, ATT_Q:o_att], wl[:, o_att + 3 * dr:o_rw]], axis=1).astype(BF16)
        p = _project(xf, wp, _tile(t, 512), P_COLS // 3)

        y_attn = _attention(p, attn_sinks[l], tables, w_attn_o[l].astype(BF16), batch, seq)

        mu = rwkv_mu[l]
        prm = {
            "mu_rkv": mu[:3 * dr].reshape(3, dr),
            "mu_lora": mu[3 * dr:].reshape(1, -1),
            "w0": rwkv_w0[l].reshape(1, dr), "a0": rwkv_a0[l].reshape(1, dr),
            "w2": _pad_rows(rwkv_w2[l], LANES, 0).astype(BF16),
            "a2": _pad_rows(rwkv_a2[l], LANES, DECAY_LORA).astype(BF16),
            "g2": rwkv_g2[l].astype(BF16),
        }
        if l > 0:
            prm["v0"] = rwkv_v0[l - 1].reshape(1, dr)
            prm["v1"] = jnp.zeros((dr, LANES), F32).at[:, :MV_LORA].set(rwkv_v1[l - 1]).astype(BF16)
            prm["v2"] = _pad_rows(rwkv_v2[l - 1], LANES, 0).astype(BF16)
        scan_prm = jnp.stack([rwkv_k_k[l], rwkv_k_a[l], rwkv_r_k[l].reshape(dr), rwkv_lnx_w[l], rwkv_lnx_b[l],
                              jnp.zeros((dr,), F32), jnp.zeros((dr,), F32), jnp.zeros((dr,), F32)])
        z, v_ = _rwkv_mix(p, prm, scan_prm, v_first, batch, seq)
        if l == 0:
            v_first = v_

        x1, x1_tiles = _merge(z, y_attn, p, xf, w_rwkv_o[l].astype(BF16), w_out[l].astype(BF16),
                    ln1_g[l].reshape(1, d), ln1_b[l].reshape(1, d), alpha, _tile(t, 512))

        rw = jnp.zeros((d, LANES), F32).at[:, :N_EXPERTS].set(router_w[l])
        rb = jnp.full((1, LANES), -1e30, F32).at[0, :N_EXPERTS].set(router_b[l])
        top_idx, gates, rank, counts = _router(x1, jnp.stack(_split3(rw)), rb, _tile(t, 512))
        dest, row_tok, block_exp, n_used = _routing_tables(top_idx[:, :TOP_K], rank[:, :TOP_K],
                                                           counts[0, :N_EXPERTS])

        b1 = expert_b1[l]
        w1g, w1l = _deinterleave(expert_w1[l].reshape(N_EXPERTS * d, 2 * D_EXPERT), 512)
        y_rows = _experts(x1_tiles, row_tok, block_exp, n_used,
                          w1g.reshape(N_EXPERTS, d, D_EXPERT), w1l.reshape(N_EXPERTS, d, D_EXPERT),
                          b1[:, None, 0::2], b1[:, None, 1::2],
                          expert_w2[l].astype(BF16), expert_b2[l][:, None, :])
        xf = _combine(y_rows, dest, gates, x1, ln2_g[l].reshape(1, d), ln2_b[l].reshape(1, d),
                      alpha, _tile(t, 256))
    return xf.reshape(batch, seq, d)
```

```python
import functools

import jax
import jax.numpy as jnp
from jax import lax
from jax.experimental import pallas as pl
from jax.experimental.pallas import tpu as pltpu

F32 = jnp.float32
BF16 = jnp.bfloat16

D_MODEL = 1024
HEAD_DIM = 64
N_Q_HEADS = 16
N_KV_HEADS = 4
WINDOW = 128
ATTN_BLOCK = 128
ROPE_THETA = 10000.0
ATT_Q = N_Q_HEADS * HEAD_DIM
ATT_KV = N_KV_HEADS * HEAD_DIM

RWKV_HEAD = 64
RWKV_HEADS = 16
RWKV_DIM = RWKV_HEADS * RWKV_HEAD
DECAY_LORA = 64
AAA_LORA = 64
MV_LORA = 32
GATE_LORA = 128
RWKV_GN_EPS = 64e-5

N_EXPERTS = 32
TOP_K = 4
D_EXPERT = 1024
SWIGLU_LIMIT = 7.0
SWIGLU_ALPHA = 1.702
MOE_BLOCK = 256

LN_EPS = 1e-5

LANES = 128
HEAD_PAIRS = RWKV_HEADS // 2
SCAN_CHUNK = 64
SCAN_CHUNKS_PER_STEP = 4
VMEM_LIMIT = 56 * 1024 * 1024

COL_Q, COL_R, COL_K, COL_V, COL_GA, COL_GB = 0, 1024, 2048, 3072, 4096, 5120
COL_AK, COL_AV, COL_LORA = 6144, 6400, 6656
P_COLS = 6912


def _cparams(n_axes):
    return pltpu.CompilerParams(dimension_semantics=("arbitrary",) * n_axes,
                                vmem_limit_bytes=VMEM_LIMIT)


def _dot(a, b):
    return jnp.dot(a, b, preferred_element_type=F32)


def _dot_nt(a, b):
    return lax.dot_general(a, b, (((1,), (1,)), ((), ())), preferred_element_type=F32)


def _dot_tn(a, b):
    return lax.dot_general(a, b, (((0,), (0,)), ((), ())), preferred_element_type=F32)


def _sigmoid(z):
    return 1.0 / (1.0 + jnp.exp(-z))


def _layer_norm(h, g, b):
    mu = jnp.mean(h, axis=-1, keepdims=True)
    c = h - mu
    var = jnp.mean(c * c, axis=-1, keepdims=True)
    return c * lax.rsqrt(var + LN_EPS) * g + b


def _proj_kernel(x_ref, w_ref, o_ref):
    o_ref[...] = _dot(x_ref[...].astype(BF16), w_ref[...]).astype(o_ref.dtype)


def _project(x, w, tm, tn):
    m, k = x.shape
    n = w.shape[1]
    return pl.pallas_call(
        _proj_kernel,
        grid=(n // tn, m // tm),
        in_specs=[pl.BlockSpec((tm, k), lambda j, i: (i, 0)),
                  pl.BlockSpec((k, tn), lambda j, i: (0, j))],
        out_specs=pl.BlockSpec((tm, tn), lambda j, i: (i, j)),
        out_shape=jax.ShapeDtypeStruct((m, n), BF16),
        compiler_params=_cparams(2),
        name="proj",
    )(x, w)


def _rope(x, c, s1, s2):
    return x * c + pltpu.roll(x, LANES - HEAD_DIM // 2, 1) * s1 + pltpu.roll(x, HEAD_DIM // 2, 1) * s2


def _attn_kernel(sink_ref, q_ref, kc_ref, kp_ref, vc_ref, vp_ref, tc_ref, tp_ref, wo_ref, o_ref):
    n = pl.program_id(1)
    blk = ATTN_BLOCK
    lane = lax.broadcasted_iota(jnp.int32, (1, LANES), 1)
    lo = lane < HEAD_DIM
    cq, s1q, s2q = tc_ref[0], tc_ref[1], tc_ref[2]
    cp, s1p, s2p = tp_ref[0], tp_ref[1], tp_ref[2]

    row = lax.broadcasted_iota(jnp.int32, (blk, 2 * blk), 0)
    col = lax.broadcasted_iota(jnp.int32, (blk, 2 * blk), 1)
    valid = (col > row) & (col <= row + WINDOW) & ((n > 0) | (col >= blk))

    def lo_hi(t, off):
        sw = pltpu.roll(t, HEAD_DIM, 1)
        t_lo = jnp.where(lo, t if off == 0 else sw, 0.0).astype(BF16)
        t_hi = jnp.where(lo, 0.0, sw if off == 0 else t).astype(BF16)
        return t_lo, t_hi

    kv_tiles = []
    for c in range(ATT_KV // LANES):
        sl = slice(c * LANES, (c + 1) * LANES)
        kc = _rope(kc_ref[:, sl].astype(F32), cq, s1q, s2q)
        kp = _rope(kp_ref[:, sl].astype(F32), cp, s1p, s2p)
        kt = jnp.concatenate([kp, kc], axis=0)
        vt = jnp.concatenate([vp_ref[:, sl].astype(F32), vc_ref[:, sl].astype(F32)], axis=0)
        kv_tiles.append((kt, vt))

    k_m, v_st = [], []
    for g in range(N_KV_HEADS):
        kt, vt = kv_tiles[g // 2]
        k_m.append(lo_hi(kt, g % 2))
        v_st.append(jnp.concatenate(lo_hi(vt, g % 2), axis=0))
    n_pairs = N_Q_HEADS // 2
    log2e = 1.4426950408889634
    qp = [(_rope(q_ref[:, j * LANES:(j + 1) * LANES].astype(F32), cq, s1q, s2q)
           * (HEAD_DIM ** -0.5 * log2e)).astype(BF16) for j in range(n_pairs)]
    scores = [_dot_nt(qp[hh // 2], k_m[hh // 4][hh % 2]) for hh in range(N_Q_HEADS)]
    probs, rden = [], []
    for hh in range(N_Q_HEADS):
        s = jnp.where(valid, scores[hh], -1e30)
        sink = sink_ref[hh] * log2e
        m = jnp.maximum(jnp.max(s, axis=-1, keepdims=True), sink)
        p = jnp.exp2(s - m)
        rden.append(1.0 / (jnp.sum(p, axis=-1, keepdims=True) + jnp.exp2(sink - m)))
        probs.append(p.astype(BF16))
    outs = [_dot(jnp.concatenate([probs[2 * j], probs[2 * j + 1]], axis=1), v_st[j // 2])
            * jnp.where(lo, rden[2 * j], rden[2 * j + 1]) for j in range(n_pairs)]
    att = jnp.concatenate(outs, axis=1).astype(BF16)
    o_ref[...] = _dot(att, wo_ref[...]).astype(o_ref.dtype)


def _attention(p, sinks, tables, w_o, batch, seq):
    nb = seq // ATTN_BLOCK
    blk = ATTN_BLOCK
    kcol, vcol = COL_AK // ATT_KV, COL_AV // ATT_KV
    cur = lambda b, n: b * nb + n
    prev = lambda b, n: b * nb + jnp.maximum(n - 1, 0)
    return pl.pallas_call(
        _attn_kernel,
        grid=(batch, nb),
        in_specs=[
            pl.BlockSpec(memory_space=pltpu.SMEM),
            pl.BlockSpec((blk, ATT_Q), lambda b, n: (cur(b, n), COL_Q // ATT_Q)),
            pl.BlockSpec((blk, ATT_KV), lambda b, n: (cur(b, n), kcol)),
            pl.BlockSpec((blk, ATT_KV), lambda b, n: (prev(b, n), kcol)),
            pl.BlockSpec((blk, ATT_KV), lambda b, n: (cur(b, n), vcol)),
            pl.BlockSpec((blk, ATT_KV), lambda b, n: (prev(b, n), vcol)),
            pl.BlockSpec((3, blk, LANES), lambda b, n: (0, n, 0)),
            pl.BlockSpec((3, blk, LANES), lambda b, n: (0, jnp.maximum(n - 1, 0), 0)),
            pl.BlockSpec((ATT_Q, D_MODEL), lambda b, n: (0, 0)),
        ],
        out_specs=pl.BlockSpec((blk, D_MODEL), lambda b, n: (cur(b, n), 0)),
        out_shape=jax.ShapeDtypeStruct((batch * seq, D_MODEL), BF16),
        compiler_params=_cparams(2),
        name="attn",
    )(sinks, p, p, p, p, p, tables, tables, w_o)


def _rope_tables(seq):
    pos = jnp.arange(seq, dtype=F32)
    inv = ROPE_THETA ** (-jnp.arange(0, HEAD_DIM, 2, dtype=F32) / HEAD_DIM)
    ang = pos[:, None] * inv[None, :]
    cos, sin = jnp.cos(ang), jnp.sin(ang)
    zero = jnp.zeros_like(sin)
    c = jnp.tile(cos, (1, LANES // (HEAD_DIM // 2)))
    s1 = jnp.tile(jnp.concatenate([-sin, zero], axis=1), (1, LANES // HEAD_DIM))
    s2 = jnp.tile(jnp.concatenate([zero, sin], axis=1), (1, LANES // HEAD_DIM))
    return jnp.stack([c, s1, s2])


def _scan_kernel(r_ref, k_ref, v_ref, a_ref, g_ref, lw_ref, prm_ref, o_ref, state_ref):
    ch = SCAN_CHUNK
    c_id = pl.program_id(1)

    @pl.when(c_id == 0)
    def _():
        state_ref[...] = jnp.zeros_like(state_ref)

    n_chunks = lw_ref.shape[0] // ch
    chunk_rows = [slice(cc * ch, (cc + 1) * ch) for cc in range(n_chunks)]
    tri = (lax.broadcasted_iota(jnp.int32, (ch, ch), 0) >= lax.broadcasted_iota(jnp.int32, (ch, ch), 1)
           ).astype(BF16)
    e_in, e_ex, e_inv, e_end, g_tot, e_ref = [], [], [], [], [], []
    for rows in chunk_rows:
        lw = lw_ref[rows, :]
        hi = lw.astype(BF16)
        rem = lw - hi.astype(F32)
        mid = rem.astype(BF16)
        low = (rem - mid.astype(F32)).astype(BF16)
        cum = _dot(tri, hi) + _dot(tri, mid) + _dot(tri, low)
        ref_row = cum[ch // 2 - 1:ch // 2, :]
        total = cum[ch - 1:ch, :]
        e_in.append(jnp.exp(cum - ref_row))
        e_ex.append(jnp.exp(cum - lw - ref_row))
        e_inv.append(jnp.exp(ref_row - cum))
        e_end.append(jnp.exp(total - cum))
        g_tot.append(jnp.exp(total))
        e_ref.append(jnp.exp(ref_row))

    lane = lax.broadcasted_iota(jnp.int32, (1, LANES), 1)
    lo = lane < RWKV_HEAD
    r2 = lax.broadcasted_iota(jnp.int32, (2 * ch, 2 * ch), 0)
    c2 = lax.broadcasted_iota(jnp.int32, (2 * ch, 2 * ch), 1)
    same = ((r2 < ch) & (c2 < ch)) | ((r2 >= ch) & (c2 >= ch))
    tri_strict = same & (r2 > c2)
    tri_incl = same & (r2 >= c2)
    eye = jnp.where(r2 == c2, 1.0, 0.0)
    rb = lax.broadcasted_iota(jnp.int32, (2 * ch, LANES), 0)
    cb = lax.broadcasted_iota(jnp.int32, (2 * ch, LANES), 1)
    blockmask = ((rb < ch) & (cb < RWKV_HEAD)) | ((rb >= ch) & (cb >= RWKV_HEAD))

    def stack(x):
        return jnp.concatenate([jnp.where(lo, x, 0.0), jnp.where(lo, 0.0, x)], axis=0)

    def head_sums(x):
        s_e = jnp.sum(jnp.where(lo, x, 0.0), axis=-1, keepdims=True)
        s_o = jnp.sum(jnp.where(lo, 0.0, x), axis=-1, keepdims=True)
        return jnp.where(lo, s_e, s_o)

    sls = [slice(pr * LANES, (pr + 1) * LANES) for pr in range(HEAD_PAIRS)]
    items = [(cc, pr) for cc in range(n_chunks) for pr in range(HEAD_PAIRS)]
    pairs = range(len(items))

    def kmod_of(rows, sl):
        return k_ref[rows, sl].astype(F32) * (1.0 + (a_ref[rows, sl].astype(F32) - 1.0) * prm_ref[1:2, sl])

    abar, rbar, vs, lhs, rhs, upd = [], [], [], [], [], []
    for cc, pr in items:
        rows, sl = chunk_rows[cc], sls[pr]
        a = a_ref[rows, sl].astype(F32)
        kx = k_ref[rows, sl].astype(F32) * prm_ref[0:1, sl]
        kk = kx / jnp.maximum(jnp.sqrt(head_sums(kx * kx)), 1e-12)
        beta = kk * a
        kmod = kmod_of(rows, sl)
        ab = stack(-kk * e_ex[cc][:, sl]).astype(BF16)
        rb_ = stack(r_ref[rows, sl].astype(F32) * e_in[cc][:, sl]).astype(BF16)
        abar.append(ab)
        rbar.append(rb_)
        lhs.append(jnp.concatenate([ab, rb_], axis=0))
        rhs.append(jnp.concatenate([stack(beta * e_inv[cc][:, sl]).astype(BF16),
                                    stack(kmod * e_inv[cc][:, sl]).astype(BF16)], axis=0))
        upd.append(jnp.concatenate([stack(beta * e_end[cc][:, sl]).astype(BF16),
                                    stack(kmod * e_end[cc][:, sl]).astype(BF16)], axis=0))
        vs.append(stack(v_ref[rows, sl].astype(F32)).astype(BF16))

    grams = [_dot_nt(lhs[pr], rhs[pr]) for pr in pairs]
    a_ak = [jnp.where(tri_strict, gm[:2 * ch, 2 * ch:], 0.0).astype(BF16) for gm in grams]
    m_r = [jnp.concatenate([jnp.where(tri_incl, gm[2 * ch:, :2 * ch], 0.0),
                            jnp.where(tri_incl, gm[2 * ch:, 2 * ch:], 0.0)], axis=1).astype(BF16) for gm in grams]

    a_ab = [jnp.where(tri_strict, gm[:2 * ch, :2 * ch], 0.0) for gm in grams]
    pw = [x.astype(BF16) for x in a_ab]
    pw = [_dot(x, x).astype(BF16) for x in pw]
    tinv = [eye + x for x in a_ab]
    for _ in range((ch - 1).bit_length() - 2):
        both = [_dot(pw[pr], jnp.concatenate([pw[pr], tinv[pr].astype(BF16)], axis=1)) for pr in pairs]
        tinv = [tinv[pr] + both[pr][:, 2 * ch:] for pr in pairs]
        pw = [both[pr][:, :2 * ch].astype(BF16) for pr in pairs]
    tinv = [tinv[pr] + _dot(pw[pr], tinv[pr].astype(BF16)) for pr in pairs]

    heads = range(HEAD_PAIRS)
    st = [state_ref[pr] for pr in heads]
    o_all = []
    for cc in range(n_chunks):
        it = [cc * HEAD_PAIRS + pr for pr in heads]
        st_b = [(st[pr] * e_ref[cc][:, sls[pr]]).astype(BF16) for pr in heads]
        w = [_dot_nt(abar[it[pr]], st_b[pr]) + _dot(a_ak[it[pr]], vs[it[pr]]) for pr in heads]
        u = [_dot(tinv[it[pr]].astype(BF16), w[pr].astype(BF16)) for pr in heads]
        uv = [jnp.concatenate([u[pr].astype(BF16), vs[it[pr]]], axis=0) for pr in heads]
        o_all.append([_dot_nt(rbar[it[pr]], st_b[pr]) + _dot(m_r[it[pr]], uv[pr]) for pr in heads])
        st = [st[pr] * g_tot[cc][:, sls[pr]] + _dot_tn(uv[pr], upd[it[pr]]) for pr in heads]
    for pr in heads:
        state_ref[pr] = st[pr]

    for cc, rows in enumerate(chunk_rows):
        for pr, sl in enumerate(sls):
            o_s = o_all[cc][pr]
            mean = jnp.sum(o_s, axis=-1, keepdims=True) * (1.0 / RWKV_HEAD)
            cen = jnp.where(blockmask, o_s - mean, 0.0)
            var = jnp.sum(cen * cen, axis=-1, keepdims=True) * (1.0 / RWKV_HEAD)
            gn_s = cen * lax.rsqrt(var + RWKV_GN_EPS)
            gn = gn_s[:ch] + gn_s[ch:]
            v = v_ref[rows, sl].astype(F32)
            bonus = head_sums(r_ref[rows, sl].astype(F32) * kmod_of(rows, sl) * prm_ref[2:3, sl]) * v
            z = (gn * prm_ref[3:4, sl] + prm_ref[4:5, sl] + bonus) * g_ref[rows, sl].astype(F32)
            o_ref[rows, sl] = z.astype(o_ref.dtype)


LORA_COLS = DECAY_LORA + AAA_LORA + GATE_LORA
CARRY_COLS = 3 * RWKV_DIM + LORA_COLS


def _rwkv_kernel(has_vres, *refs):
    if has_vres:
        (r_in, k_in, v_in, l_in, mu_ref, mul_ref, w0_ref, a0_ref, w2_ref, a2_ref, g2_ref,
         vf_ref, v0_ref, v1_ref, v2_ref, prm_ref,
         o_ref, vout_ref, state_ref, carry_ref, r_s, k_s, v_s, a_s, g_s, lw_s) = refs
    else:
        (r_in, k_in, v_in, l_in, mu_ref, mul_ref, w0_ref, a0_ref, w2_ref, a2_ref, g2_ref, prm_ref,
         o_ref, vout_ref, state_ref, carry_ref, r_s, k_s, v_s, a_s, g_s, lw_s) = refs
    rows = r_in.shape[0]
    dr = RWKV_DIM

    @pl.when(pl.program_id(1) == 0)
    def _():
        carry_ref[...] = jnp.zeros_like(carry_ref)

    row0 = lax.broadcasted_iota(jnp.int32, (rows, 1), 0) == 0

    def shift_mix(cur_ref, col, mu):
        cur = cur_ref[...].astype(F32)
        last = carry_ref[0:1, col:col + cur.shape[1]]
        sh = jnp.where(row0, last, pltpu.roll(cur, 1, 0))
        return cur + (sh - cur) * mu, cur[rows - 1:rows, :]

    r, r_last = shift_mix(r_in, 0, mu_ref[0:1, :])
    k, k_last = shift_mix(k_in, dr, mu_ref[1:2, :])
    v, v_last = shift_mix(v_in, 2 * dr, mu_ref[2:3, :])
    xl, l_last = shift_mix(l_in, 3 * dr, mul_ref[...])
    carry_ref[0:1, 0:dr] = r_last
    carry_ref[0:1, dr:2 * dr] = k_last
    carry_ref[0:1, 2 * dr:3 * dr] = v_last
    carry_ref[0:1, 3 * dr:] = l_last

    wa = xl[:, :LANES]
    zw = w0_ref[...] + _dot(jnp.tanh(wa).astype(BF16), w2_ref[...])
    lw_s[...] = -jnp.exp(-0.5) * _sigmoid(zw)
    a_s[...] = _sigmoid(a0_ref[...] + _dot(wa.astype(BF16), a2_ref[...]))
    g_s[...] = _dot(_sigmoid(xl[:, LANES:]).astype(BF16), g2_ref[...])
    if has_vres:
        lo_rank = _dot(v.astype(BF16), v1_ref[...])
        mix = _sigmoid(v0_ref[...] + _dot(lo_rank.astype(BF16), v2_ref[...]))
        v = v + (vf_ref[...].astype(F32) - v) * mix
    r_s[...] = r
    k_s[...] = k
    v_s[...] = v
    vout_ref[...] = v.astype(vout_ref.dtype)
    _scan_kernel(r_s, k_s, v_s, a_s, g_s, lw_s, prm_ref, o_ref, state_ref)


def _rwkv_mix(p, prm, scan_prm, v_first, batch, seq):
    rows = SCAN_CHUNK * SCAN_CHUNKS_PER_STEP
    nc = seq // rows
    dr = RWKV_DIM
    has_vres = v_first is not None
    full = lambda a: pl.BlockSpec(a.shape, lambda b, c: (0,) * a.ndim)
    col = lambda cb: (lambda b, c: (b * nc + c, cb))
    in_specs = [pl.BlockSpec((rows, dr), col(COL_R // dr)),
                pl.BlockSpec((rows, dr), col(COL_K // dr)),
                pl.BlockSpec((rows, dr), col(COL_V // dr)),
                pl.BlockSpec((rows, LORA_COLS), col(COL_LORA // LORA_COLS))]
    args = [p] * 4
    small = [prm["mu_rkv"], prm["mu_lora"], prm["w0"], prm["a0"], prm["w2"], prm["a2"], prm["g2"]]
    if has_vres:
        small_v = [prm["v0"], prm["v1"], prm["v2"]]
        in_specs += [full(a) for a in small] + [pl.BlockSpec((rows, dr), col(0))] + [full(a) for a in small_v]
        args += small + [v_first] + small_v
    else:
        in_specs += [full(a) for a in small]
        args += small
    in_specs.append(full(scan_prm))
    args.append(scan_prm)
    blk = pl.BlockSpec((rows, dr), col(0))
    out = jax.ShapeDtypeStruct((batch * seq, dr), BF16)
    return pl.pallas_call(
        functools.partial(_rwkv_kernel, has_vres),
        grid=(batch, nc),
        in_specs=in_specs,
        out_specs=[blk, blk],
        out_shape=[out, out],
        scratch_shapes=[pltpu.VMEM((HEAD_PAIRS, 2 * RWKV_HEAD, LANES), F32),
                        pltpu.VMEM((SUBLANES, CARRY_COLS), F32)] + [pltpu.VMEM((rows, dr), F32)] * 6,
        compiler_params=_cparams(2),
        name="rwkv_mix",
    )(*args)


def _merge_kernel(alpha, z_ref, ya_ref, ga_ref, gb_ref, x_ref, wr_ref, wo_ref, lg_ref, lb_ref, o_ref, ot_ref):
    y_r = _dot(z_ref[...], wr_ref[...])
    mix = _sigmoid(ga_ref[...].astype(F32)) * ya_ref[...].astype(F32) + _sigmoid(gb_ref[...].astype(F32)) * y_r
    mixed = _dot(mix.astype(BF16), wo_ref[...])
    out = _layer_norm(alpha * x_ref[...] + mixed, lg_ref[...], lb_ref[...])
    o_ref[...] = out
    _store_token_rows(ot_ref, out)


def _merge(z, y_attn, p, x, w_r, w_o, ln_g, ln_b, alpha, tm):
    t, d = x.shape
    row = lambda i: (i, 0)
    full = lambda a: pl.BlockSpec(a.shape, lambda i: (0,) * a.ndim)
    return pl.pallas_call(
        functools.partial(_merge_kernel, alpha),
        grid=(t // tm,),
        in_specs=[pl.BlockSpec((tm, d), row), pl.BlockSpec((tm, d), row),
                  pl.BlockSpec((tm, d), lambda i: (i, COL_GA // d)),
                  pl.BlockSpec((tm, d), lambda i: (i, COL_GB // d)),
                  pl.BlockSpec((tm, d), row), full(w_r), full(w_o), full(ln_g), full(ln_b)],
        out_specs=[pl.BlockSpec((tm, d), row), pl.BlockSpec((tm * SUBLANES, LANES), row)],
        out_shape=[jax.ShapeDtypeStruct((t, d), F32), jax.ShapeDtypeStruct((t * SUBLANES, LANES), F32)],
        compiler_params=_cparams(1),
        name="merge_ln1",
    )(z, y_attn, p, p, x, w_r, w_o, ln_g, ln_b)


def _split3(x):
    hi = x.astype(BF16)
    rem = x - hi.astype(F32)
    mid = rem.astype(BF16)
    low = (rem - mid.astype(F32)).astype(BF16)
    return hi, mid, low


def _router_kernel(x_ref, w_ref, b_ref, idx_ref, gate_ref, rank_ref, cnt_ref, carry_ref):
    @pl.when(pl.program_id(0) == 0)
    def _():
        carry_ref[...] = jnp.zeros_like(carry_ref)

    x0, x1, x2 = _split3(x_ref[...])
    w0, w1, w2 = w_ref[0], w_ref[1], w_ref[2]
    logits = (_dot(x0, w0) + (_dot(x0, w1) + _dot(x1, w0)) +
              (_dot(x0, w2) + _dot(x1, w1) + _dot(x2, w0))) + b_ref[...]
    tm = logits.shape[0]
    lane = lax.broadcasted_iota(jnp.int32, (tm, LANES), 1).astype(F32)
    work = logits
    idx_out = jnp.zeros((tm, LANES), F32)
    val_out = jnp.zeros((tm, LANES), F32)
    top = None
    den = jnp.zeros((tm, 1), F32)
    hits = []
    for kk in range(TOP_K):
        m = jnp.max(work, axis=-1, keepdims=True)
        sel = jnp.min(jnp.where(work == m, lane, float(LANES)), axis=-1, keepdims=True)
        if kk == 0:
            top = m
        e = jnp.exp(m - top)
        den = den + e
        idx_out = jnp.where(lane == float(kk), sel, idx_out)
        val_out = jnp.where(lane == float(kk), e, val_out)
        hits.append(lane == sel)
        work = jnp.where(hits[-1], -jnp.inf, work)
    idx_ref[...] = idx_out.astype(jnp.int32)
    gate_ref[...] = val_out / den

    chosen = jnp.where(hits[0] | hits[1] | hits[2] | hits[3], 1.0, 0.0)
    r_i = lax.broadcasted_iota(jnp.int32, (tm, tm), 0)
    c_i = lax.broadcasted_iota(jnp.int32, (tm, tm), 1)
    earlier = jnp.where(r_i > c_i, 1.0, 0.0).astype(BF16)
    before = carry_ref[0:1, :] + _dot(earlier, chosen.astype(BF16))
    rank_out = jnp.zeros((tm, LANES), F32)
    for kk in range(TOP_K):
        rk = jnp.sum(jnp.where(hits[kk], before, 0.0), axis=-1, keepdims=True)
        rank_out = jnp.where(lane == float(kk), rk, rank_out)
    rank_ref[...] = rank_out.astype(jnp.int32)
    carry_ref[0:1, :] = carry_ref[0:1, :] + jnp.sum(chosen, axis=0, keepdims=True)
    cnt_ref[...] = carry_ref[...].astype(jnp.int32)


def _router(x, w3, b, tm):
    t, d = x.shape
    row = lambda i: (i, 0)
    return pl.pallas_call(
        _router_kernel,
        grid=(t // tm,),
        in_specs=[pl.BlockSpec((tm, d), row), pl.BlockSpec(w3.shape, lambda i: (0, 0, 0)),
                  pl.BlockSpec(b.shape, lambda i: (0, 0))],
        out_specs=[pl.BlockSpec((tm, LANES), row)] * 3 + [pl.BlockSpec((8, LANES), lambda i: (0, 0))],
        out_shape=[jax.ShapeDtypeStruct((t, LANES), jnp.int32), jax.ShapeDtypeStruct((t, LANES), F32),
                   jax.ShapeDtypeStruct((t, LANES), jnp.int32), jax.ShapeDtypeStruct((8, LANES), jnp.int32)],
        scratch_shapes=[pltpu.VMEM((8, LANES), F32)],
        compiler_params=_cparams(1),
        name="router",
    )(x, w3, b)


SUBLANES = 8
TOKEN_TILES = D_MODEL // LANES


def _row_copy(src_hbm, dst_vmem, sem, src_row8, dst_tok):
    if not isinstance(src_row8, int):
        src_row8 = pl.multiple_of(src_row8, SUBLANES)
    return pltpu.make_async_copy(src_hbm.at[pl.ds(src_row8, SUBLANES), :],
                                 dst_vmem.at[pl.ds(dst_tok * SUBLANES, SUBLANES), :], sem)


def _load_token_rows(ref, n):
    return jnp.concatenate([ref[pl.ds(s, n, stride=SUBLANES), :] for s in range(TOKEN_TILES)], axis=1)


def _store_token_rows(ref, val):
    n = val.shape[0]
    for s in range(TOKEN_TILES):
        ref[pl.ds(s, n, stride=SUBLANES), :] = val[:, s * LANES:(s + 1) * LANES]


DEINT_GROUP = 2 * LANES


def _deint_kernel(w_ref, sel_ref, g_ref, l_ref):
    for c in range(w_ref.shape[1] // DEINT_GROUP):
        blk = w_ref[:, c * DEINT_GROUP:(c + 1) * DEINT_GROUP].astype(BF16)
        both = _dot(blk, sel_ref[...])
        g_ref[:, c * LANES:(c + 1) * LANES] = both[:, :LANES].astype(BF16)
        l_ref[:, c * LANES:(c + 1) * LANES] = both[:, LANES:].astype(BF16)


def _deinterleave(w, tm):
    rows, cols = w.shape
    src = jnp.arange(DEINT_GROUP)[:, None]
    dst = jnp.arange(DEINT_GROUP)[None, :]
    sel = jnp.where(dst < LANES, src == 2 * dst, src == 2 * (dst - LANES) + 1).astype(BF16)
    half = jax.ShapeDtypeStruct((rows, cols // 2), BF16)
    return pl.pallas_call(
        _deint_kernel,
        grid=(rows // tm,),
        in_specs=[pl.BlockSpec((tm, cols), lambda i: (i, 0)), pl.BlockSpec(sel.shape, lambda i: (0, 0))],
        out_specs=[pl.BlockSpec((tm, cols // 2), lambda i: (i, 0))] * 2,
        out_shape=[half, half],
        compiler_params=_cparams(1),
        name="deinterleave",
    )(w, sel)


def _expert_kernel(be_ref, nu_ref, idx0_ref, idxn_ref, x_hbm, w1g_ref, w1l_ref, b1g_ref, b1l_ref, w2_ref, b2_ref,
                   o_ref, xbuf, sems):
    i = pl.program_id(0)
    last = pl.num_programs(0) - 1
    slot = lax.rem(i, 2)
    used = i < nu_ref[0]

    def issue(idx_ref, s):
        for r in range(MOE_BLOCK):
            _row_copy(x_hbm, xbuf.at[s], sems.at[s], idx_ref[0, 0, r], r).start(priority=r % 2)

    def drain(s):
        for r in range(MOE_BLOCK):
            _row_copy(x_hbm, xbuf.at[s], sems.at[s], 0, r).wait()

    @pl.when(i == 0)
    def _():
        issue(idx0_ref, 0)

    issue(idxn_ref, 1 - slot)
    drain(slot)

    @pl.when(i == last)
    def _():
        drain(1 - slot)

    @pl.when(jnp.logical_not(used))
    def _():
        o_ref[...] = jnp.zeros_like(o_ref)

    @pl.when(used)
    def _():
        xb = _load_token_rows(xbuf.at[slot], MOE_BLOCK).astype(BF16)
        h_glu = _dot(xb, w1g_ref[...]) + b1g_ref[...]
        h_lin = _dot(xb, w1l_ref[...]) + b1l_ref[...]
        x_glu = jnp.minimum(h_glu, SWIGLU_LIMIT)
        x_lin = jnp.clip(h_lin, -SWIGLU_LIMIT, SWIGLU_LIMIT)
        act = x_glu * _sigmoid(SWIGLU_ALPHA * x_glu) * (x_lin + 1.0)
        _store_token_rows(o_ref, _dot(act.astype(BF16), w2_ref[...]) + b2_ref[...])


def _experts(x_tiles, row_tok, block_exp, n_used, w1g, w1l, b1g, b1l, w2, b2):
    d = D_MODEL
    nblk = row_tok.shape[0] // MOE_BLOCK
    de = w1g.shape[2]
    idx = (row_tok * SUBLANES).reshape(nblk, 1, MOE_BLOCK)
    blk_rows = MOE_BLOCK * SUBLANES
    wmap = lambda i, be, nu: (be[i], 0, 0)
    grid_spec = pltpu.PrefetchScalarGridSpec(
        num_scalar_prefetch=2,
        grid=(nblk,),
        in_specs=[pl.BlockSpec((1, 1, MOE_BLOCK), lambda i, be, nu: (0, 0, 0), memory_space=pltpu.SMEM),
                  pl.BlockSpec((1, 1, MOE_BLOCK), lambda i, be, nu: (jnp.minimum(i + 1, nblk - 1), 0, 0),
                               memory_space=pltpu.SMEM),
                  pl.BlockSpec(memory_space=pl.ANY),
                  pl.BlockSpec((None, d, de), wmap), pl.BlockSpec((None, d, de), wmap),
                  pl.BlockSpec((None, 1, de), wmap), pl.BlockSpec((None, 1, de), wmap),
                  pl.BlockSpec((None, de, d), wmap), pl.BlockSpec((None, 1, d), wmap)],
        out_specs=pl.BlockSpec((blk_rows, LANES), lambda i, be, nu: (i, 0)),
        scratch_shapes=[pltpu.VMEM((2, blk_rows, LANES), F32), pltpu.SemaphoreType.DMA((2,))],
    )
    return pl.pallas_call(
        _expert_kernel,
        grid_spec=grid_spec,
        out_shape=jax.ShapeDtypeStruct((nblk * blk_rows, LANES), F32),
        compiler_params=_cparams(1),
        name="moe_experts",
    )(block_exp, n_used, idx, idx, x_tiles, w1g, w1l, b1g, b1l, w2, b2)


def _combine_kernel(alpha, idx0_ref, idxn_ref, y_hbm, gate_ref, x_ref, lg_ref, lb_ref, o_ref, buf, sems):
    tc = x_ref.shape[0]
    i = pl.program_id(0)
    last = pl.num_programs(0) - 1
    slot = lax.rem(i, 2)

    def issue(idx_ref, s):
        for r in range(tc):
            for kk in range(TOP_K):
                _row_copy(y_hbm, buf.at[s, kk], sems.at[s], idx_ref[0, 0, r * TOP_K + kk], r).start(priority=kk % 2)

    def drain(s):
        for r in range(tc * TOP_K):
            _row_copy(y_hbm, buf.at[s, 0], sems.at[s], 0, 0).wait()

    @pl.when(i == 0)
    def _():
        issue(idx0_ref, 0)

    issue(idxn_ref, 1 - slot)
    drain(slot)

    @pl.when(i == last)
    def _():
        drain(1 - slot)

    gates = gate_ref[...]
    y = _load_token_rows(buf.at[slot, 0], tc) * gates[:, 0:1]
    for kk in range(1, TOP_K):
        y = y + _load_token_rows(buf.at[slot, kk], tc) * gates[:, kk:kk + 1]
    o_ref[...] = _layer_norm(alpha * x_ref[...] + y, lg_ref[...], lb_ref[...])


def _combine(y_rows, dest, gates, x, ln_g, ln_b, alpha, tc):
    t, d = x.shape
    steps = t // tc
    row = lambda i: (i, 0)
    full = lambda a: pl.BlockSpec(a.shape, lambda i: (0,) * a.ndim)
    idx = (dest * SUBLANES).reshape(steps, 1, tc * TOP_K)
    return pl.pallas_call(
        functools.partial(_combine_kernel, alpha),
        grid=(steps,),
        in_specs=[pl.BlockSpec((1, 1, tc * TOP_K), lambda i: (0, 0, 0), memory_space=pltpu.SMEM),
                  pl.BlockSpec((1, 1, tc * TOP_K), lambda i: (jnp.minimum(i + 1, steps - 1), 0, 0),
                               memory_space=pltpu.SMEM),
                  pl.BlockSpec(memory_space=pl.ANY),
                  pl.BlockSpec((tc, LANES), row), pl.BlockSpec((tc, d), row), full(ln_g), full(ln_b)],
        out_specs=pl.BlockSpec((tc, d), row),
        out_shape=jax.ShapeDtypeStruct((t, d), F32),
        scratch_shapes=[pltpu.VMEM((2, TOP_K, tc * SUBLANES, LANES), F32), pltpu.SemaphoreType.DMA((2,))],
        compiler_params=_cparams(1),
        name="moe_combine_ln2",
    )(idx, idx, y_rows, gates, x, ln_g, ln_b)


def _routing_tables(top_idx, rank, counts):
    t = top_idx.shape[0]
    n_assign = t * TOP_K
    e_flat = top_idx.reshape(n_assign)
    padded = ((counts + MOE_BLOCK - 1) // MOE_BLOCK) * MOE_BLOCK
    pend = jnp.cumsum(padded)
    pstart = pend - padded
    dest = (pstart[e_flat] + rank.reshape(n_assign)).astype(jnp.int32)
    n_blocks = -(-n_assign // MOE_BLOCK) + N_EXPERTS
    tok_flat = jnp.arange(n_assign, dtype=jnp.int32) // TOP_K
    row_tok = jnp.zeros((n_blocks * MOE_BLOCK,), jnp.int32).at[dest].set(
        tok_flat, unique_indices=True, mode="promise_in_bounds")
    block_start = jnp.arange(n_blocks, dtype=jnp.int32) * MOE_BLOCK
    block_exp = jnp.minimum(jnp.sum(pend[None, :] <= block_start[:, None], axis=1), N_EXPERTS - 1).astype(jnp.int32)
    n_used = (pend[-1] // MOE_BLOCK).astype(jnp.int32).reshape(1)
    return dest, row_tok, block_exp, n_used


def _pad_rows(w, rows, at):
    out = jnp.zeros((rows, w.shape[1]), w.dtype)
    return out.at[at:at + w.shape[0]].set(w)


def _tile(n, pref):
    return pref if n % pref == 0 else n


def kernel(x, w_in, attn_sinks, w_attn_o, rwkv_mu, rwkv_w0, rwkv_w2, rwkv_a0, rwkv_a2, rwkv_g2, rwkv_k_k, rwkv_k_a, rwkv_r_k, rwkv_v0, rwkv_v1, rwkv_v2, rwkv_lnx_w, rwkv_lnx_b, w_rwkv_o, w_out, ln1_g, ln1_b, router_w, router_b, expert_w1, expert_b1, expert_w2, expert_b2, ln2_g, ln2_b):
    batch, seq, d = x.shape
    depth = w_in.shape[0]
    t = batch * seq
    alpha = (2.0 * depth) ** 0.25
    dr = RWKV_DIM
    tables = _rope_tables(seq)
    xf = x.reshape(t, d)
    v_first = None
    o_att, o_rw = ATT_Q + 2 * ATT_KV, ATT_Q + 2 * ATT_KV + 3 * dr + DECAY_LORA + AAA_LORA + GATE_LORA

    for l in range(depth):
        wl = w_in[l]
        wp = jnp.concatenate([
            wl[:, :ATT_Q], wl[:, o_att:o_att + 3 * dr], wl[:, o_rw:o_rw + 2 * d],
            wl[:, ATT_Q:o_att], wl[:, o_att + 3 * dr:o_rw]], axis=1).astype(BF16)
        p = _project(xf, wp, _tile(t, 512), P_COLS // 3)

        y_attn = _attention(p, attn_sinks[l], tables, w_attn_o[l].astype(BF16), batch, seq)

        mu = rwkv_mu[l]
        prm = {
            "mu_rkv": mu[:3 * dr].reshape(3, dr),
            "mu_lora": mu[3 * dr:].reshape(1, -1),
            "w0": rwkv_w0[l].reshape(1, dr), "a0": rwkv_a0[l].reshape(1, dr),
            "w2": _pad_rows(rwkv_w2[l], LANES, 0).astype(BF16),
            "a2": _pad_rows(rwkv_a2[l], LANES, DECAY_LORA).astype(BF16),
            "g2": rwkv_g2[l].astype(BF16),
        }
        if l > 0:
            prm["v0"] = rwkv_v0[l - 1].reshape(1, dr)
            prm["v1"] = jnp.zeros((dr, LANES), F32).at[:, :MV_LORA].set(rwkv_v1[l - 1]).astype(BF16)
            prm["v2"] = _pad_rows(rwkv_v2[l - 1], LANES, 0).astype(BF16)
        scan_prm = jnp.stack([rwkv_k_k[l], rwkv_k_a[l], rwkv_r_k[l].reshape(dr), rwkv_lnx_w[l], rwkv_lnx_b[l],
                              jnp.zeros((dr,), F32), jnp.zeros((dr,), F32), jnp.zeros((dr,), F32)])
        z, v_ = _rwkv_mix(p, prm, scan_prm, v_first, batch, seq)
        if l == 0:
            v_first = v_

        x1, x1_tiles = _merge(z, y_attn, p, xf, w_rwkv_o[l].astype(BF16), w_out[l].astype(BF16),
                    ln1_g[l].reshape(1, d), ln1_b[l].reshape(1, d), alpha, _tile(t, 512))

        rw = jnp.zeros((d, LANES), F32).at[:, :N_EXPERTS].set(router_w[l])
        rb = jnp.full((1, LANES), -1e30, F32).at[0, :N_EXPERTS].set(router_b[l])
        top_idx, gates, rank, counts = _router(x1, jnp.stack(_split3(rw)), rb, _tile(t, 512))
        dest, row_tok, block_exp, n_used = _routing_tables(top_idx[:, :TOP_K], rank[:, :TOP_K],
                                                           counts[0, :N_EXPERTS])

        b1 = expert_b1[l]
        w1g, w1l = _deinterleave(expert_w1[l].reshape(N_EXPERTS * d, 2 * D_EXPERT), 512)
        y_rows = _experts(x1_tiles, row_tok, block_exp, n_used,
                          w1g.reshape(N_EXPERTS, d, D_EXPERT), w1l.reshape(N_EXPERTS, d, D_EXPERT),
                          b1[:, None, 0::2], b1[:, None, 1::2],
                          expert_w2[l].astype(BF16), expert_b2[l][:, None, :])
        xf = _combine(y_rows, dest, gates, x1, ln2_g[l].reshape(1, d), ln2_b[l].reshape(1, d),
                      alpha, _tile(t, 256))
    return xf.reshape(batch, seq, d)
```
